```python
import math
import jax
import jax.numpy as jnp
from jax import lax
import numpy as np

D_MODEL = 1024
BATCH = 2
SEQ = 16384
DEPTH = 4

CTX_LEN = 256
GRID_W = 64
D_S5 = D_MODEL // 2
S5_GROUP = 16
S5_GROUPS = D_S5 // S5_GROUP
S5_STATE = 64
D_HGRN = D_MODEL // 2
HGRN_EXPAND = 128
HGRN_HEADS = D_HGRN // HGRN_EXPAND
HGRN_DK = HGRN_EXPAND
HGRN_DV = D_HGRN // HGRN_HEADS
HGRN_CHUNK = 64
N_IN = D_S5 + 5 * D_HGRN + 2 * D_MODEL
N_EXPERTS = 32
TOP_K = 4
D_FF = D_MODEL
SWIGLU_LIMIT = 7.0
SWIGLU_ALPHA = 1.702
MOE_BLOCK = 128
N_MOD = 6
EPS = 1e-6

kernel_name = 'hybrid_s5_hgrn2_moe_flow_trunk'


def _rms_norm(x, g):
    xf = x.astype(jnp.float32)
    y = xf * lax.rsqrt(jnp.mean(xf * xf, axis=-1, keepdims=True) + EPS) * g.astype(jnp.float32)
    return y.astype(x.dtype)


def _to_colmajor(h, rows):
    b, n, d = h.shape
    return h.reshape(b, rows, GRID_W, d).transpose(0, 2, 1, 3).reshape(b, n, d)


def _from_colmajor(h, rows):
    b, n, d = h.shape
    return h.reshape(b, GRID_W, rows, d).transpose(0, 2, 1, 3).reshape(b, n, d)


def _split_in(z):
    sizes = (D_S5, D_HGRN, D_HGRN, D_HGRN, D_HGRN, D_HGRN, D_MODEL, D_MODEL)
    return jnp.split(z, [int(i) for i in np.cumsum(sizes)[:-1]], axis=-1)


def _complex_affine_combine(e1, e2):
    a1r, a1i, b1r, b1i = e1
    a2r, a2i, b2r, b2i = e2
    return (a2r * a1r - a2i * a1i,
            a2r * a1i + a2i * a1r,
            a2r * b1r - a2i * b1i + b2r,
            a2r * b1i + a2i * b1r + b2i)


def _s5_discretise(a_re, a_im, log_dt, b_re, b_im):
    a_re, a_im, b_re, b_im = (t.astype(jnp.float32) for t in (a_re, a_im, b_re, b_im))
    dt = jnp.exp(log_dt.astype(jnp.float32))[:, None]
    mag = jnp.exp(a_re * dt)
    lbr = mag * jnp.cos(a_im * dt)
    lbi = mag * jnp.sin(a_im * dt)
    den = a_re * a_re + a_im * a_im
    nr = lbr - 1.0
    qr = ((nr * a_re + lbi * a_im) / den)[..., None]
    qi = ((lbi * a_re - nr * a_im) / den)[..., None]
    return lbr, lbi, qr * b_re - qi * b_im, qr * b_im + qi * b_re


def _s5_scan(ur, ui, lbr, lbi):
    n = ur.shape[1]
    ar = jnp.broadcast_to(lbr, (1, n) + lbr.shape)
    ai = jnp.broadcast_to(lbi, (1, n) + lbi.shape)
    _, _, xr, xi = lax.associative_scan(_complex_affine_combine, (ar, ai, ur, ui), axis=1)
    return xr, xi


def _s5_direction(uc, ul, lbr, lbi, bbr, bbi, cr, ci, need_ctx):
    def drive(u):
        return (jnp.einsum('blgh,gph->blgp', u, bbr), jnp.einsum('blgh,gph->blgp', u, bbi))

    def readout(xr, xi):
        return (jnp.einsum('blgp,ghp->blgh', xr, cr.astype(jnp.float32))
                - jnp.einsum('blgp,ghp->blgh', xi, ci.astype(jnp.float32)))

    xcr, xci = _s5_scan(*drive(uc), lbr, lbi)
    x0r, x0i = xcr[:, -1], xci[:, -1]
    ur, ui = drive(ul)
    ur = ur.at[:, 0].add(lbr * x0r - lbi * x0i)
    ui = ui.at[:, 0].add(lbr * x0i + lbi * x0r)
    xlr, xli = _s5_scan(ur, ui, lbr, lbi)
    yc = readout(xcr, xci) if need_ctx else None
    return yc, readout(xlr, xli)


def _s5_branch(uc, ul, a_re, a_im, log_dt, b_re, b_im, c_re, c_im, d_skip, w_glu, need_ctx):
    dtype = ul.dtype

    def grp(u):
        b, n, _ = u.shape
        return u.astype(jnp.float32).reshape(b, n, S5_GROUPS, S5_GROUP)

    ucg, ulg = grp(uc), grp(ul)
    yc, yl = None, None
    for d in range(2):
        lbr, lbi, bbr, bbi = _s5_discretise(a_re[d], a_im[d], log_dt[d], b_re[d], b_im[d])
        ucd, uld = (ucg, ulg) if d == 0 else (jnp.flip(ucg, axis=1), jnp.flip(ulg, axis=1))
        ycd, yld = _s5_direction(ucd, uld, lbr, lbi, bbr, bbi, c_re[d], c_im[d], need_ctx)
        if d == 1:
            yld = jnp.flip(yld, axis=1)
            ycd = jnp.flip(ycd, axis=1) if need_ctx else None
        yl = yld if yl is None else yl + yld
        if need_ctx:
            yc = ycd if yc is None else yc + ycd
    d_g = d_skip.astype(jnp.float32).reshape(S5_GROUPS, S5_GROUP)
    wg = w_glu.astype(jnp.float32)

    def finish(y, u):
        b, n = y.shape[:2]
        z = jax.nn.gelu((y + d_g * u).reshape(b, n, D_S5))
        return (z * jax.nn.sigmoid(z @ wg)).astype(dtype)

    return (finish(yc, ucg) if need_ctx else None), finish(yl, ulg)


def _heads(t):
    b, n, _ = t.shape
    return t.astype(jnp.float32).reshape(b, n, HGRN_HEADS, -1).transpose(0, 2, 1, 3)


def _hgrn2_gates(f_logit, lb):
    lb = lb.reshape(HGRN_HEADS, 1, HGRN_DK)
    logf = jnp.log(lb + (1.0 - lb) * jax.nn.sigmoid(f_logit))
    k = (1.0 - lb) * jax.nn.sigmoid(-f_logit)
    return logf, k


def _hgrn2_chunk_scan(q, k, v, logf, s0):
    b, h, n, _ = q.shape
    nc = n // HGRN_CHUNK

    def blocks(t):
        return jnp.moveaxis(t.reshape(b, h, nc, HGRN_CHUNK, t.shape[-1]), 2, 0)

    tril = jnp.tril(jnp.ones((HGRN_CHUNK, HGRN_CHUNK), dtype=bool))[:, :, None]
    tril_f = tril.astype(jnp.float32)

    def step(state, inp):
        qc, kc, vc, gc = inp
        cum = jnp.cumsum(gc, axis=2)
        rel = cum[:, :, :, None, :] - cum[:, :, None, :, :]
        decay = jnp.exp(jnp.where(tril, rel, 0.0)) * tril_f
        scores = jnp.einsum('bhtd,bhsd,bhtsd->bhts', qc, kc, decay)
        o = (jnp.einsum('bhts,bhse->bhte', scores, vc)
             + jnp.einsum('bhtd,bhde->bhte', qc * jnp.exp(cum), state))
        tot = cum[:, :, -1:, :]
        state = (jnp.exp(tot)[:, :, 0, :, None] * state
                 + jnp.einsum('bhsd,bhse->bhde', kc * jnp.exp(tot - cum), vc))
        return state, o

    s_fin, o = lax.scan(step, s0, (blocks(q), blocks(k), blocks(v), blocks(logf)))
    return s_fin, jnp.moveaxis(o, 0, 2).reshape(b, h, n, v.shape[-1])


def _hgrn2_readout(o, g, norm_g, dtype):
    o = o * lax.rsqrt(jnp.mean(o * o, axis=-1, keepdims=True) + EPS) * norm_g.astype(jnp.float32)
    b, h, n, dv = o.shape
    o = o.transpose(0, 2, 1, 3).reshape(b, n, h * dv)
    return (o * jax.nn.silu(g.astype(jnp.float32))).astype(dtype)


def _hgrn2_branch(qc, ic, fcf, fcb, gc, ql, il, flf, flb, gl, lb, norm_g, need_ctx):
    dtype = ql.dtype
    qch, ich, qlh, ilh = _heads(qc), _heads(ic), _heads(ql), _heads(il)
    b = qlh.shape[0]
    s0 = jnp.zeros((b, HGRN_HEADS, HGRN_DK, HGRN_DV), jnp.float32)
    oc, ol = None, None
    for d, (fc, fl) in enumerate(((fcf, flf), (fcb, flb))):
        logf_c, k_c = _hgrn2_gates(_heads(fc), lb[d])
        logf_l, k_l = _hgrn2_gates(_heads(fl), lb[d])
        args_c = (qch, k_c, ich, logf_c)
        args_l = (qlh, k_l, ilh, logf_l)
        if d == 1:
            args_c = tuple(jnp.flip(t, axis=2) for t in args_c)
            args_l = tuple(jnp.flip(t, axis=2) for t in args_l)
        s_ctx, ocd = _hgrn2_chunk_scan(*args_c, s0)
        _, old = _hgrn2_chunk_scan(*args_l, s_ctx)
        if d == 1:
            ocd, old = jnp.flip(ocd, axis=2), jnp.flip(old, axis=2)
        oc = ocd if oc is None else oc + ocd
        ol = old if ol is None else ol + old
    yc = _hgrn2_readout(oc, gc, norm_g, dtype) if need_ctx else None
    return yc, _hgrn2_readout(ol, gl, norm_g, dtype)


def _token_mixer(hc, hl, need_ctx, w_in, a_re, a_im, log_dt, b_re, b_im, c_re, c_im, d_skip,
                 w_glu, lb, norm_g, w_up_s5, w_up_hgrn, w_out):
    zc = _split_in(hc @ w_in)
    zl = _split_in(hl @ w_in)
    y5c, y5l = _s5_branch(zc[0], zl[0], a_re, a_im, log_dt, b_re, b_im, c_re, c_im,
                          d_skip, w_glu, need_ctx)
    yhc, yhl = _hgrn2_branch(zc[1], zc[2], zc[3], zc[4], zc[5],
                             zl[1], zl[2], zl[3], zl[4], zl[5], lb, norm_g, need_ctx)

    def merge(z, y5, yh):
        return (jax.nn.sigmoid(z[6]) * (y5 @ w_up_s5)
                + jax.nn.sigmoid(z[7]) * (yh @ w_up_hgrn)) @ w_out

    yc = merge(zc, y5c, yhc) if need_ctx else None
    return yc, merge(zl, y5l, yhl)


def _moe_ffn(h, router_w, router_b, w1, b1, w2, b2):
    n, d = h.shape
    logits = (h @ router_w + router_b).astype(jnp.float32)
    top_val, top_idx = lax.top_k(logits, TOP_K)
    gate = jax.nn.softmax(top_val, axis=-1).astype(h.dtype)
    nk = n * TOP_K
    flat_e = top_idx.reshape(nk).astype(jnp.int32)
    flat_tok = jnp.repeat(jnp.arange(n, dtype=jnp.int32), TOP_K)
    order = jnp.argsort(flat_e)
    sorted_e = flat_e[order]
    counts = jnp.bincount(flat_e, length=N_EXPERTS)
    padded = (counts + MOE_BLOCK - 1) // MOE_BLOCK * MOE_BLOCK
    pad_end = jnp.cumsum(padded)
    pad_start = pad_end - padded
    grp_start = jnp.cumsum(counts) - counts
    dest = pad_start[sorted_e] + jnp.arange(nk, dtype=jnp.int32) - grp_start[sorted_e]
    n_rows = -(-nk // MOE_BLOCK) * MOE_BLOCK + N_EXPERTS * MOE_BLOCK
    n_blocks = n_rows // MOE_BLOCK
    row_tok = jnp.full((n_rows,), n, jnp.int32).at[dest].set(flat_tok[order])
    row_gate = jnp.zeros((n_rows,), h.dtype).at[dest].set(gate.reshape(nk)[order])
    block_e = jnp.minimum(
        jnp.searchsorted(pad_end, jnp.arange(n_blocks, dtype=jnp.int32) * MOE_BLOCK, side='right'),
        N_EXPERTS - 1)
    xs = jnp.concatenate([h, jnp.zeros((1, d), h.dtype)], axis=0)[row_tok]
    xs = xs.reshape(n_blocks, MOE_BLOCK, d)

    def expert_block(args):
        xb, e = args
        gu = xb @ w1[e] + b1[e]
        g, u = jnp.split(gu, 2, axis=-1)
        g = jnp.minimum(g, SWIGLU_LIMIT)
        u = jnp.clip(u, -SWIGLU_LIMIT, SWIGLU_LIMIT)
        return (g * jax.nn.sigmoid(SWIGLU_ALPHA * g) * (u + 1.0)) @ w2[e] + b2[e]

    ys = lax.map(expert_block, (xs, block_e)).reshape(n_rows, d)
    return jax.ops.segment_sum(ys * row_gate[:, None], row_tok, num_segments=n + 1)[:n]


def setup_inputs(seed: int = 0) -> dict:
    key = jax.random.key(seed)
    ks = jax.random.split(key, 32)

    def nrm(k, shape, s):
        return s * jax.random.normal(k, shape, jnp.float32)

    G, P, H = S5_GROUPS, S5_STATE, S5_GROUP
    n_idx = jnp.arange(P, dtype=jnp.float32)
    return {
        'x': nrm(ks[0], (BATCH, SEQ, D_MODEL), 1.0),
        'c': nrm(ks[1], (BATCH, D_MODEL), 1.0),
        'ctx': nrm(ks[2], (BATCH, CTX_LEN, D_MODEL), 1.0),
        'c_ctx': nrm(ks[3], (D_MODEL,), 1.0),
        'mod_w': nrm(ks[4], (DEPTH, D_MODEL, N_MOD * D_MODEL), 0.2 * D_MODEL ** -0.5),
        'mod_b': nrm(ks[5], (DEPTH, N_MOD * D_MODEL), 0.02),
        'norm1_g': 1.0 + nrm(ks[6], (DEPTH, D_MODEL), 0.02),
        'norm2_g': 1.0 + nrm(ks[7], (DEPTH, D_MODEL), 0.02),
        'w_in': nrm(ks[8], (DEPTH, D_MODEL, N_IN), D_MODEL ** -0.5),
        's5_a_re': -0.5 + nrm(ks[9], (DEPTH, 2, G, P), 0.02),
        's5_a_im': math.pi * n_idx + nrm(ks[10], (DEPTH, 2, G, P), 0.02),
        's5_log_dt': jax.random.uniform(ks[11], (DEPTH, 2, G), jnp.float32,
                                        math.log(1e-3), math.log(1e-1)),
        's5_b_re': nrm(ks[12], (DEPTH, 2, G, P, H), (2 * H) ** -0.5),
        's5_b_im': nrm(ks[13], (DEPTH, 2, G, P, H), (2 * H) ** -0.5),
        's5_c_re': nrm(ks[14], (DEPTH, 2, G, H, P), (2 * P) ** -0.5),
        's5_c_im': nrm(ks[15], (DEPTH, 2, G, H, P), (2 * P) ** -0.5),
        's5_d': nrm(ks[16], (DEPTH, D_S5), 1.0),
        's5_w_glu': nrm(ks[17], (DEPTH, D_S5, D_S5), D_S5 ** -0.5),
        'hgrn_lb_logits': nrm(ks[18], (2, DEPTH + 1, D_HGRN), 0.1),
        'hgrn_norm_g': 1.0 + nrm(ks[19], (DEPTH, HGRN_DV), 0.02),
        'w_up_s5': nrm(ks[20], (DEPTH, D_S5, D_MODEL), D_S5 ** -0.5),
        'w_up_hgrn': nrm(ks[21], (DEPTH, D_HGRN, D_MODEL), D_HGRN ** -0.5),
        'w_out': nrm(ks[22], (DEPTH, D_MODEL, D_MODEL), D_MODEL ** -0.5),
        'router_w': nrm(ks[23], (DEPTH, D_MODEL, N_EXPERTS), D_MODEL ** -0.5),
        'router_b': nrm(ks[24], (DEPTH, N_EXPERTS), 0.01),
        'moe_w1': nrm(ks[25], (DEPTH, N_EXPERTS, D_MODEL, 2 * D_FF), D_MODEL ** -0.5),
        'moe_b1': nrm(ks[26], (DEPTH, N_EXPERTS, 2 * D_FF), 0.01),
        'moe_w2': nrm(ks[27], (DEPTH, N_EXPERTS, D_FF, D_MODEL), D_FF ** -0.5),
        'moe_b2': nrm(ks[28], (DEPTH, N_EXPERTS, D_MODEL), 0.01),
        'final_g': 1.0 + nrm(ks[29], (D_MODEL,), 0.02),
    }


def reference(x, c, ctx, c_ctx, mod_w, mod_b, norm1_g, norm2_g, w_in,
              s5_a_re, s5_a_im, s5_log_dt, s5_b_re, s5_b_im, s5_c_re, s5_c_im, s5_d, s5_w_glu,
              hgrn_lb_logits, hgrn_norm_g, w_up_s5, w_up_hgrn, w_out,
              router_w, router_b, moe_w1, moe_b1, moe_w2, moe_b2, final_g):
    b, n_lat, d = x.shape
    n_ctx = ctx.shape[1]
    rows = n_lat // GRID_W
    p = jax.nn.softmax(hgrn_lb_logits.astype(jnp.float32), axis=1)
    lb_all = jnp.cumsum(p, axis=1)[:, :DEPTH]
    sc = jax.nn.silu(c)
    scc = jax.nn.silu(c_ctx)
    xc = ctx
    for l in range(DEPTH):
        need_ctx = l < DEPTH - 1
        ml = jnp.split((sc @ mod_w[l] + mod_b[l])[:, None, :], N_MOD, axis=-1)
        mc = jnp.split(scc @ mod_w[l] + mod_b[l], N_MOD, axis=-1)
        hl = _rms_norm(x, norm1_g[l]) * (1.0 + ml[1]) + ml[0]
        hc = _rms_norm(xc, norm1_g[l]) * (1.0 + mc[1]) + mc[0]
        if l % 2 == 1:
            hl = _to_colmajor(hl, rows)
        yc, yl = _token_mixer(hc, hl, need_ctx, w_in[l], s5_a_re[l], s5_a_im[l], s5_log_dt[l],
                              s5_b_re[l], s5_b_im[l], s5_c_re[l], s5_c_im[l], s5_d[l],
                              s5_w_glu[l], lb_all[:, l], hgrn_norm_g[l],
                              w_up_s5[l], w_up_hgrn[l], w_out[l])
        if l % 2 == 1:
            yl = _from_colmajor(yl, rows)
        x = x + ml[2] * yl
        hl = _rms_norm(x, norm2_g[l]) * (1.0 + ml[4]) + ml[3]
        if need_ctx:
            xc = xc + mc[2] * yc
            hc = _rms_norm(xc, norm2_g[l]) * (1.0 + mc[4]) + mc[3]
            tokens = jnp.concatenate([hc.reshape(-1, d), hl.reshape(-1, d)], axis=0)
            y = _moe_ffn(tokens, router_w[l], router_b[l], moe_w1[l], moe_b1[l],
                         moe_w2[l], moe_b2[l])
            xc = xc + mc[5] * y[: b * n_ctx].reshape(b, n_ctx, d)
            x = x + ml[5] * y[b * n_ctx:].reshape(b, n_lat, d)
        else:
            y = _moe_ffn(hl.reshape(-1, d), router_w[l], router_b[l], moe_w1[l], moe_b1[l],
                         moe_w2[l], moe_b2[l])
            x = x + ml[5] * y.reshape(b, n_lat, d)
    return _rms_norm(x, final_g)
```

```python
import functools
import math

import jax
import jax.numpy as jnp
from jax import lax
from jax.experimental import pallas as pl
from jax.experimental.pallas import tpu as pltpu

F32 = jnp.float32
BF16 = jnp.bfloat16

D = 1024
GRID_W = 64
D_S5 = 512
S5_H = 16
S5_G = D_S5 // S5_H
S5_P = 64
D_HG = 512
HG_HEADS = 4
HG_DK = 128
N_IN = D_S5 + 5 * D_HG + 2 * D
N_EXP = 32
TOP_K = 4
D_FF = D
SWIGLU_LIMIT = 7.0
SWIGLU_ALPHA = 1.702
N_MOD = 6
EPS = 1e-6

LANES = 128
TM = 256
S5_T = 32
S5_TH = S5_T * S5_H
HG_C = 64
HG_SUB = 16
HG_CLAMP = 80.0
MOE_BLK = 256
VMEM_LIMIT = 56 * 1024 * 1024
NEG = -1e30


def _cparams(sem, vmem=VMEM_LIMIT):
    return pltpu.CompilerParams(dimension_semantics=sem, vmem_limit_bytes=vmem)


def _rms(x, g):
    return x * lax.rsqrt(jnp.mean(x * x, axis=-1, keepdims=True) + EPS) * g


def _split3(x):
    hi = x.astype(BF16)
    r1 = x - hi.astype(F32)
    mid = r1.astype(BF16)
    lo = (r1 - mid.astype(F32)).astype(BF16)
    return hi, mid, lo


def _mod_kernel(c_ref, w_ref, b_ref, o_ref):
    c = c_ref[...]
    sc = c * jax.nn.sigmoid(c)
    acc = jnp.zeros(o_ref.shape[1:], F32)
    w = w_ref[0]
    for piece in _split3(sc):
        for wp in _split3(w):
            acc = acc + jnp.dot(piece, wp, preferred_element_type=F32)
    o_ref[0] = acc + b_ref[0]


def _modulation(cpad, mod_w, mod_b):
    depth = mod_w.shape[0]
    tn = 1536
    return pl.pallas_call(
        _mod_kernel,
        grid=(depth, (N_MOD * D) // tn),
        in_specs=[
            pl.BlockSpec((8, D), lambda l, j: (0, 0)),
            pl.BlockSpec((1, D, tn), lambda l, j: (l, 0, j)),
            pl.BlockSpec((1, 1, tn), lambda l, j: (l, 0, j)),
        ],
        out_specs=pl.BlockSpec((1, 8, tn), lambda l, j: (l, 0, j)),
        out_shape=jax.ShapeDtypeStruct((depth, 8, N_MOD * D), F32),
        compiler_params=_cparams(("arbitrary", "arbitrary")),
        name="modulation",
    )(cpad, mod_w, mod_b.reshape(depth, 1, N_MOD * D))


def _inproj_kernel(x_ref, m_ref, g_ref, w_ref, z_ref):
    m = m_ref[0]
    h = _rms(x_ref[...], g_ref[...])
    h = h * (1.0 + m[:, D:2 * D]) + m[:, 0:D]
    z_ref[...] = jnp.dot(h.astype(BF16), w_ref[...], preferred_element_type=F32)


def _inproj(xall, mods3, g, w_bf, mod_row):
    n = xall.shape[0]
    return pl.pallas_call(
        _inproj_kernel,
        grid=(n // TM,),
        in_specs=[
            pl.BlockSpec((TM, D), lambda i: (i, 0)),
            pl.BlockSpec((1, 1, N_MOD * D), lambda i: (mod_row(i), 0, 0)),
            pl.BlockSpec((1, D), lambda i: (0, 0)),
            pl.BlockSpec((D, N_IN), lambda i: (0, 0)),
        ],
        out_specs=pl.BlockSpec((TM, N_IN), lambda i: (i, 0)),
        out_shape=jax.ShapeDtypeStruct((n, N_IN), F32),
        compiler_params=_cparams(("arbitrary",)),
        name="inproj",
    )(xall, mods3, g.reshape(1, D), w_bf)


def _s5_kernel(u_ref, mcat_ref, vcat_ref, lam_ref, y_ref, r_scr, x_scr, *, nb, nc, nctx):
    th = S5_TH
    r = jnp.dot(u_ref[0], mcat_ref[0], preferred_element_type=F32)
    y_ref[0] = r[:, :th]
    r_scr[...] = r[:, th:]
    lam = lam_ref[0]
    a1f, a2f, a3f = lam[0:1], lam[1:2], lam[2:3]
    a1b, a2b, a3b = lam[3:4], lam[4:5], lam[5:6]
    sub = 8
    zero = jnp.zeros((sub, LANES), F32)
    srow = lax.broadcasted_iota(jnp.int32, (sub, LANES), 0)
    ntile, nctile = nc // sub, nctx // sub

    def bcast(tile, s, lo):
        return jnp.broadcast_to(tile[s:s + 1, lo:lo + LANES], (sub, LANES))

    def body(j, carry):
        tf = j
        tb = jnp.where(j < nctile, nctile - 1 - j, ntile - 1 - (j - nctile))
        out = []
        for b in range(nb):
            xf, xsf, xb, xsb = carry[4 * b:4 * b + 4]
            basef = pl.multiple_of(b * nc + tf * sub, sub)
            baseb = pl.multiple_of(b * nc + tb * sub, sub)
            dtf = r_scr[pl.ds(basef, sub), 0:2 * LANES]
            dtb = r_scr[pl.ds(baseb, sub), 2 * LANES:4 * LANES]
            accf, accb = zero, zero
            for s in range(sub):
                accf = jnp.where(srow == s, xf, accf)
                xf, xsf = (a1f * xf + a2f * xsf + bcast(dtf, s, 0),
                           a1f * xsf + a3f * xf + bcast(dtf, s, LANES))
                sb = sub - 1 - s
                accb = jnp.where(srow == sb, xb, accb)
                xb, xsb = (a1b * xb + a2b * xsb + bcast(dtb, sb, 0),
                           a1b * xsb + a3b * xb + bcast(dtb, sb, LANES))
            x_scr[pl.ds(basef, sub), 0:LANES] = accf
            x_scr[pl.ds(baseb, sub), LANES:2 * LANES] = accb
            out += [xf, xsf, xb, xsb]
        return tuple(out)

    lax.fori_loop(0, ntile, body, (zero,) * (4 * nb))
    xs = x_scr[...]
    xhi = xs.astype(BF16)
    xlo = (xs - xhi.astype(F32)).astype(BF16)
    v = vcat_ref[0]
    y_ref[0] += (jnp.dot(xhi, v, preferred_element_type=F32)
                 + jnp.dot(xlo, v, preferred_element_type=F32))


def _s5_scan(ug, mcat, vcat, lam, nb, nc, nctx):
    rows = nb * nc
    kern = functools.partial(_s5_kernel, nb=nb, nc=nc, nctx=nctx)
    return pl.pallas_call(
        kern,
        grid=(S5_G,),
        in_specs=[
            pl.BlockSpec((1, rows, S5_TH), lambda g: (g, 0, 0)),
            pl.BlockSpec((1, S5_TH, S5_TH + 4 * LANES), lambda g: (g, 0, 0)),
            pl.BlockSpec((1, 2 * LANES, S5_TH), lambda g: (g, 0, 0)),
            pl.BlockSpec((1, 8, LANES), lambda g: (g, 0, 0)),
        ],
        out_specs=pl.BlockSpec((1, rows, S5_TH), lambda g: (g, 0, 0)),
        out_shape=jax.ShapeDtypeStruct((S5_G, rows, S5_TH), F32),
        scratch_shapes=[pltpu.VMEM((rows, 4 * LANES), F32), pltpu.VMEM((rows, 2 * LANES), F32)],
        compiler_params=_cparams(("arbitrary",)),
        name="s5_scan",
    )(ug, mcat, vcat, lam)


def _s5_matrices(a_re, a_im, log_dt, b_re, b_im, c_re, c_im):
    t = S5_T
    dt = jnp.exp(log_dt.astype(F32))[..., None]
    are, aim = a_re.astype(F32), a_im.astype(F32)

    def lam_pow(tau):
        tau = tau.astype(F32)[None, :, None, None]
        mag = jnp.exp(are[:, None] * dt[:, None] * tau)
        ang = aim[:, None] * dt[:, None] * tau
        return mag * jnp.cos(ang), mag * jnp.sin(ang)

    mag1 = jnp.exp(are * dt)
    lbr, lbi = mag1 * jnp.cos(aim * dt), mag1 * jnp.sin(aim * dt)
    den = are * are + aim * aim
    nr = lbr - 1.0
    qr = ((nr * are + lbi * aim) / den)[..., None]
    qi = ((lbi * are - nr * aim) / den)[..., None]
    bre, bim = b_re.astype(F32), b_im.astype(F32)
    bbr = qr * bre - qi * bim
    bbi = qr * bim + qi * bre
    cre, cim = c_re.astype(F32), c_im.astype(F32)

    pr, pi = lam_pow(jnp.arange(t + 1))
    cl_r = cre[:, None] * pr[:, :, :, None, :] - cim[:, None] * pi[:, :, :, None, :]
    cl_i = cre[:, None] * pi[:, :, :, None, :] + cim[:, None] * pr[:, :, :, None, :]
    kk = (jnp.einsum('dtgop,dgpi->dtgoi', cl_r[:, :t], bbr)
          - jnp.einsum('dtgop,dgpi->dtgoi', cl_i[:, :t], bbi))
    kf, kb = kk[0], kk[1]
    s_idx = jnp.arange(t)[:, None]
    t_idx = jnp.arange(t)[None, :]
    lag_f = t_idx - s_idx
    lag_b = s_idx - t_idx
    mf = jnp.where((lag_f >= 0)[:, :, None, None, None], kf[jnp.clip(lag_f, 0, t - 1)], 0.0)
    mb = jnp.where((lag_b >= 0)[:, :, None, None, None], kb[jnp.clip(lag_b, 0, t - 1)], 0.0)
    m = (mf + mb).transpose(2, 0, 4, 1, 3).reshape(S5_G, S5_TH, S5_TH)

    pf_r, pf_i = pr[0, :t][::-1], pi[0, :t][::-1]
    pb_r, pb_i = pr[1, :t], pi[1, :t]

    def wmat(p_r, p_i, br, bi):
        wr = p_r[:, :, :, None] * br[None] - p_i[:, :, :, None] * bi[None]
        wi = p_r[:, :, :, None] * bi[None] + p_i[:, :, :, None] * br[None]
        to = lambda w: w.transpose(1, 0, 3, 2).reshape(S5_G, S5_TH, S5_P)
        return to(wr), to(wi)

    wfr, wfi = wmat(pf_r, pf_i, bbr[0], bbi[0])
    wbr, wbi = wmat(pb_r, pb_i, bbr[1], bbi[1])
    mcat = jnp.concatenate([m, wfr, wfi, wfi, wfr, wbr, wbi, wbi, wbr], axis=-1).astype(BF16)

    def vmat(clr, cli):
        vr = clr.transpose(1, 3, 0, 2).reshape(S5_G, S5_P, S5_TH)
        vi = (-cli).transpose(1, 3, 0, 2).reshape(S5_G, S5_P, S5_TH)
        return jnp.concatenate([vr, vi], axis=1)

    vf = vmat(cl_r[0, 1:t + 1], cl_i[0, 1:t + 1])
    vb = vmat(cl_r[1, 1:t + 1][::-1], cl_i[1, 1:t + 1][::-1])
    vcat = jnp.concatenate([vf, vb], axis=1).astype(BF16)

    ltr, lti = pr[:, t], pi[:, t]
    rows = []
    for d in range(2):
        rows += [jnp.concatenate([ltr[d], ltr[d]], -1),
                 jnp.concatenate([-lti[d], lti[d]], -1),
                 jnp.concatenate([lti[d], -lti[d]], -1)]
    rows += [jnp.zeros_like(rows[0])] * 2
    lam = jnp.stack(rows, axis=1)
    return mcat, vcat, lam


def _s5_branch(z, layer_odd, nb, seq, nctx_tok, mats):
    mcat, vcat, lam = mats
    t = S5_T
    n_lat = nb * seq
    u_lat = z[:n_lat, :D_S5]
    u_ctx = z[n_lat:, :D_S5]
    rows = seq // GRID_W
    if layer_odd:
        ul = u_lat.reshape(nb, rows // t, t, GRID_W, S5_G, S5_H).transpose(4, 0, 3, 1, 2, 5)
    else:
        ul = u_lat.reshape(nb, seq // t, t, S5_G, S5_H).transpose(3, 0, 1, 2, 4)
    ul = ul.reshape(S5_G, nb, seq // t, S5_TH)
    uc = u_ctx.reshape(nb, nctx_tok // t, t, S5_G, S5_H).transpose(3, 0, 1, 2, 4)
    uc = uc.reshape(S5_G, nb, nctx_tok // t, S5_TH)
    nctx = nctx_tok // t
    nc = nctx + seq // t
    ug = jnp.concatenate([uc, ul], axis=2).astype(BF16).reshape(S5_G, nb * nc, S5_TH)
    yg = _s5_scan(ug, mcat, vcat, lam, nb, nc, nctx).reshape(S5_G, nb, nc, t, S5_H)
    yc = yg[:, :, :nctx].transpose(1, 2, 3, 0, 4).reshape(nb * nctx_tok, D_S5)
    yl = yg[:, :, nctx:]
    if layer_odd:
        yl = yl.reshape(S5_G, nb, GRID_W, rows // t, t, S5_H).transpose(1, 3, 4, 2, 0, 5)
    else:
        yl = yl.transpose(1, 2, 3, 0, 4)
    return jnp.concatenate([yl.reshape(n_lat, D_S5), yc], axis=0)


def _hgrn_dir(q_ref, v_ref, f_ref, lbv, o_ref, s_scr, d, backward):
    c = HG_C
    x = f_ref[...]
    f = lbv + (1.0 - lbv) * jax.nn.sigmoid(x)
    logf = jnp.log(f)
    kk = (1.0 - lbv) * jax.nn.sigmoid(-x)
    row = lax.broadcasted_iota(jnp.int32, (c, c), 0)
    col = lax.broadcasted_iota(jnp.int32, (c, c), 1)
    keep = (col >= row) if backward else (col <= row)
    tri = keep.astype(BF16)
    cum = jnp.zeros((c, D_HG), F32)
    for piece in _split3(logf):
        cum = cum + jnp.dot(tri, piece, preferred_element_type=F32)
    tot = cum[0:1] if backward else cum[c - 1:c]
    q = q_ref[...]
    v = v_ref[...]
    qe = q * jnp.exp(cum)
    kh = kk * jnp.exp(tot - cum)
    etot = jnp.exp(tot)
    nsub = c // HG_SUB
    for h in range(HG_HEADS):
        sl = slice(h * HG_DK, (h + 1) * HG_DK)
        qh, kh_h, vh, cumh = q[:, sl], kk[:, sl], v[:, sl], cum[:, sl]
        vb16 = vh.astype(BF16)
        st = s_scr[d, h]
        inter = lax.dot_general(qe[:, sl].astype(BF16), st.astype(BF16),
                                (((1,), (1,)), ((), ())), preferred_element_type=F32)
        blocks = []
        for i in range(nsub):
            lo, hi = i * HG_SUB, (i + 1) * HG_SUB
            if backward:
                ref = cumh[hi:hi + 1] if i < nsub - 1 else jnp.zeros((1, HG_DK), F32)
            else:
                ref = cumh[lo - 1:lo] if i > 0 else jnp.zeros((1, HG_DK), F32)
            qt = qh[lo:hi] * jnp.exp(cumh[lo:hi] - ref)
            kt = kh_h * jnp.exp(jnp.minimum(ref - cumh, HG_CLAMP))
            blocks.append(lax.dot_general(qt.astype(BF16), kt.astype(BF16),
                                          (((1,), (1,)), ((), ())), preferred_element_type=F32))
        scores = jnp.where(keep, jnp.concatenate(blocks, axis=0), 0.0)
        o_ref[:, sl] = inter + jnp.dot(scores.astype(BF16), vb16, preferred_element_type=F32)
        upd = lax.dot_general(vb16, kh[:, sl].astype(BF16),
                              (((0,), (0,)), ((), ())), preferred_element_type=F32)
        s_scr[d, h] = st * etot[:, sl] + upd


def _hgrn_kernel(qf_ref, vf_ref, ff_ref, qb_ref, vb_ref, fb_ref, lb_ref, s0_ref,
                 of_ref, ob_ref, sfin_ref, s_scr):
    i = pl.program_id(1)

    @pl.when(i == 0)
    def _():
        s_scr[...] = s0_ref[0]

    _hgrn_dir(qf_ref, vf_ref, ff_ref, lb_ref[0:1], of_ref, s_scr, 0, False)
    _hgrn_dir(qb_ref, vb_ref, fb_ref, lb_ref[1:2], ob_ref, s_scr, 1, True)

    @pl.when(i == pl.num_programs(1) - 1)
    def _():
        sfin_ref[0] = s_scr[...]


def _hgrn_scan(zview, lb2, s0, nb, nsteps, in_f, in_b, out_f, out_b, out_view_shape):
    blk = (HG_C, D_HG)
    state_shape = (nb, 2, HG_HEADS, HG_DK, HG_DK)
    state_spec = pl.BlockSpec((1,) + state_shape[1:], lambda b, i: (b, 0, 0, 0, 0))
    in_specs = [
        pl.BlockSpec(blk, lambda b, i: in_f(b, i, 1)),
        pl.BlockSpec(blk, lambda b, i: in_f(b, i, 2)),
        pl.BlockSpec(blk, lambda b, i: in_f(b, i, 3)),
        pl.BlockSpec(blk, lambda b, i: in_b(b, i, 1)),
        pl.BlockSpec(blk, lambda b, i: in_b(b, i, 2)),
        pl.BlockSpec(blk, lambda b, i: in_b(b, i, 4)),
        pl.BlockSpec((2, D_HG), lambda b, i: (0, 0)),
        state_spec,
    ]
    return pl.pallas_call(
        _hgrn_kernel,
        grid=(nb, nsteps),
        in_specs=in_specs,
        out_specs=[pl.BlockSpec(blk, out_f), pl.BlockSpec(blk, out_b), state_spec],
        out_shape=[jax.ShapeDtypeStruct(out_view_shape, F32),
                   jax.ShapeDtypeStruct(out_view_shape, F32),
                   jax.ShapeDtypeStruct(state_shape, F32)],
        scratch_shapes=[pltpu.VMEM(state_shape[1:], F32)],
        compiler_params=_cparams(("arbitrary", "arbitrary")),
        name="hgrn_scan",
    )(*([zview] * 6 + [lb2, s0]))


def _hgrn_branch(z, layer_odd, nb, seq, nctx_tok, lb2):
    n = z.shape[0]
    n_lat = nb * seq
    c = HG_C
    ncols = N_IN // D_HG
    nsc = nctx_tok // c
    base_c = n_lat // c
    s0 = jnp.zeros((nb, 2, HG_HEADS, HG_DK, HG_DK), F32)
    oc_f, oc_b, s_ctx = _hgrn_scan(
        z, lb2, s0, nb, nsc,
        lambda b, i, cb: (base_c + b * nsc + i, cb),
        lambda b, i, cb: (base_c + b * nsc + (nsc - 1 - i), cb),
        lambda b, i: (b * nsc + i, 0),
        lambda b, i: (b * nsc + (nsc - 1 - i), 0),
        (nb * nctx_tok, D_HG))
    nsl = seq // c
    if layer_odd:
        rpc = seq // GRID_W // c
        zview = z.reshape(n // GRID_W, GRID_W * N_IN)
        oview = (n_lat // GRID_W, GRID_W * D_HG)

        def lat_in(b, i, cb):
            return (b * rpc + i % rpc, (i // rpc) * ncols + cb)

        def lat_out(b, i):
            return (b * rpc + i % rpc, i // rpc)
    else:
        zview = z
        oview = (n_lat, D_HG)

        def lat_in(b, i, cb):
            return (b * nsl + i, cb)

        def lat_out(b, i):
            return (b * nsl + i, 0)

    ol_f, ol_b, _ = _hgrn_scan(
        zview, lb2, s_ctx, nb, nsl,
        lat_in, lambda b, i, cb: lat_in(b, nsl - 1 - i, cb),
        lat_out, lambda b, i: lat_out(b, nsl - 1 - i), oview)
    return ol_f.reshape(n_lat, D_HG), ol_b.reshape(n_lat, D_HG), oc_f, oc_b


def _gelu_tanh(x):
    return 0.5 * x * (1.0 + jnp.tanh(math.sqrt(2.0 / math.pi) * (x + 0.044715 * (x * x * x))))


def _post_kernel(x_ref, u_ref, g_ref, gs_ref, gh_ref, y5_ref, ofl_ref, obl_ref, ofc_ref, obc_ref,
                 m_ref, d_ref, wglu_ref, hn_ref, wus_ref, wuh_ref, wo_ref, n2_ref,
                 rwh_ref, rwl_ref, rb_ref,
                 xo_ref, h2_ref, gate_ref, meta_ref, cnt_ref, run_scr, *, n_lat_tiles):
    i = pl.program_id(0)

    @pl.when(i == 0)
    def _():
        run_scr[...] = jnp.zeros_like(run_scr)

    m = m_ref[0]
    zz = _gelu_tanh(y5_ref[...] + d_ref[...] * u_ref[...])
    glu = jnp.dot(zz.astype(BF16), wglu_ref[...], preferred_element_type=F32)
    y5 = zz * jax.nn.sigmoid(glu)
    o = jnp.where(i < n_lat_tiles, ofl_ref[...] + obl_ref[...], ofc_ref[...] + obc_ref[...])
    hn = hn_ref[...]
    on = jnp.concatenate(
        [_rms(o[:, h * HG_DK:(h + 1) * HG_DK], hn) for h in range(HG_HEADS)], axis=1)
    g = g_ref[...]
    yh = on * (g * jax.nn.sigmoid(g))
    merged = (jax.nn.sigmoid(gs_ref[...])
              * jnp.dot(y5.astype(BF16), wus_ref[...], preferred_element_type=F32)
              + jax.nn.sigmoid(gh_ref[...])
              * jnp.dot(yh.astype(BF16), wuh_ref[...], preferred_element_type=F32))
    y = jnp.dot(merged.astype(BF16), wo_ref[...], preferred_element_type=F32)
    xn = x_ref[...] + m[:, 2 * D:3 * D] * y
    xo_ref[...] = xn
    h2 = _rms(xn, n2_ref[...]) * (1.0 + m[:, 4 * D:5 * D]) + m[:, 3 * D:4 * D]
    h2_ref[...] = h2

    hi = h2.astype(BF16)
    lo = (h2 - hi.astype(F32)).astype(BF16)
    rwh = rwh_ref[...]
    logits = (jnp.dot(hi, rwh, preferred_element_type=F32)
              + jnp.dot(lo, rwh, preferred_element_type=F32)
              + jnp.dot(hi, rwl_ref[...], preferred_element_type=F32)) + rb_ref[...]
    tm = logits.shape[0]
    lane = lax.broadcasted_iota(jnp.int32, (tm, LANES), 1)
    lane_f = lane.astype(F32)
    l = logits
    vals, idxs, hots = [], [], []
    for _ in range(TOP_K):
        mk = jnp.max(l, axis=1, keepdims=True)
        ik = jnp.min(jnp.where(l == mk, lane_f, float(LANES)), axis=1, keepdims=True)
        hot = lane_f == ik
        vals.append(mk)
        idxs.append(ik)
        hots.append(hot)
        l = jnp.where(hot, NEG * 10.0, l)
    exps = [jnp.exp(vk - vals[0]) for vk in vals]
    den = exps[0] + exps[1] + exps[2] + exps[3]
    sel = (hots[0] | hots[1] | hots[2] | hots[3]).astype(F32)
    r_i = lax.broadcasted_iota(jnp.int32, (tm, tm), 0)
    c_i = lax.broadcasted_iota(jnp.int32, (tm, tm), 1)
    strict = (c_i < r_i).astype(BF16)
    pos = jnp.dot(strict, sel.astype(BF16), preferred_element_type=F32) + run_scr[...]
    run_scr[...] += jnp.sum(sel, axis=0, keepdims=True)
    cnt_ref[...] = run_scr[...]
    gate = jnp.zeros((tm, LANES), F32)
    meta = jnp.zeros((tm, LANES), jnp.int32)
    for k in range(TOP_K):
        pk = jnp.sum(jnp.where(hots[k], pos, 0.0), axis=1, keepdims=True).astype(jnp.int32)
        gate = jnp.where(lane == k, exps[k] / den, gate)
        meta = jnp.where(lane == k, idxs[k].astype(jnp.int32), meta)
        meta = jnp.where(lane == TOP_K + k, pk, meta)
    gate_ref[...] = gate
    meta_ref[...] = meta


def _post(xall, z, y5s, hg_out, mods3, mod_row, p, n_lat_tiles):
    n = xall.shape[0]
    tile = lambda w, cb: pl.BlockSpec((TM, w), lambda i: (i, cb))
    full = lambda a: pl.BlockSpec(a.shape, lambda i: (0,) * a.ndim)
    lat_tile = pl.BlockSpec((TM, D_HG), lambda i: (jnp.minimum(i, n_lat_tiles - 1), 0))
    ctx_tile = pl.BlockSpec((TM, D_HG), lambda i: (jnp.maximum(i - n_lat_tiles, 0), 0))
    consts = [p['d'], p['wglu'], p['hn'], p['wus'], p['wuh'], p['wo'], p['n2'],
              p['rwh'], p['rwl'], p['rb']]
    return pl.pallas_call(
        functools.partial(_post_kernel, n_lat_tiles=n_lat_tiles),
        grid=(n // TM,),
        in_specs=[
            tile(D, 0),
            tile(D_S5, 0),
            tile(D_HG, 5),
            tile(D, 3),
            tile(D, 4),
            tile(D_S5, 0), lat_tile, lat_tile, ctx_tile, ctx_tile,
            pl.BlockSpec((1, 1, N_MOD * D), lambda i: (mod_row(i), 0, 0)),
        ] + [full(a) for a in consts],
        out_specs=[tile(D, 0), tile(D, 0), tile(LANES, 0), tile(LANES, 0),
                   pl.BlockSpec((1, LANES), lambda i: (0, 0))],
        out_shape=[jax.ShapeDtypeStruct((n, D), F32), jax.ShapeDtypeStruct((n, D), F32),
                   jax.ShapeDtypeStruct((n, LANES), F32), jax.ShapeDtypeStruct((n, LANES), jnp.int32),
                   jax.ShapeDtypeStruct((1, LANES), F32)],
        scratch_shapes=[pltpu.VMEM((1, LANES), F32)],
        compiler_params=_cparams(("arbitrary",)),
        name="mixer_post",
    )(xall, z, z, z, z, y5s, *hg_out, mods3, *consts)


def _dispatch_kernel(dest_ref, h_ref, xs_in, xs_ref, sem):
    del xs_in

    def body(r, carry):
        for k in range(TOP_K):
            dst = dest_ref[0, 0, r * TOP_K + k]
            pltpu.make_async_copy(h_ref.at[pl.ds(r, 1)], xs_ref.at[pl.ds(dst, 1)], sem).start()
        return carry

    lax.fori_loop(0, TM, body, 0)
    for _ in range(TOP_K):
        pltpu.make_async_copy(h_ref, h_ref, sem).wait()


def _dispatch(h2, dest3, xs0):
    n = h2.shape[0]
    return pl.pallas_call(
        _dispatch_kernel,
        grid=(n // TM,),
        in_specs=[
            pl.BlockSpec((1, 1, TM * TOP_K), lambda i: (i, 0, 0), memory_space=pltpu.SMEM),
            pl.BlockSpec((TM, D), lambda i: (i, 0)),
            pl.BlockSpec(memory_space=pl.ANY),
        ],
        out_specs=pl.BlockSpec(memory_space=pl.ANY),
        out_shape=jax.ShapeDtypeStruct(xs0.shape, F32),
        scratch_shapes=[pltpu.SemaphoreType.DMA(())],
        input_output_aliases={2: 0},
        compiler_params=pltpu.CompilerParams(dimension_semantics=("arbitrary",),
                                             vmem_limit_bytes=VMEM_LIMIT, has_side_effects=True),
        name="moe_dispatch",
    )(dest3, h2, xs0)


def _expert_kernel(be_ref, nu_ref, xs_ref, w1_ref, b1_ref, w2_ref, b2_ref, ys_ref):
    i = pl.program_id(0)

    @pl.when(i < nu_ref[0])
    def _():
        gu = jnp.dot(xs_ref[...].astype(BF16), w1_ref[0], preferred_element_type=F32) + b1_ref[0]
        g = jnp.minimum(gu[:, :D_FF], SWIGLU_LIMIT)
        u = jnp.clip(gu[:, D_FF:], -SWIGLU_LIMIT, SWIGLU_LIMIT)
        a = g * jax.nn.sigmoid(SWIGLU_ALPHA * g) * (u + 1.0)
        ys_ref[...] = jnp.dot(a.astype(BF16), w2_ref[0], preferred_element_type=F32) + b2_ref[0]

    @pl.when(i >= nu_ref[0])
    def _():
        ys_ref[...] = jnp.zeros_like(ys_ref)


def _experts(block_e, n_used, xs, w1, b1, w2, b2):
    n_rows = xs.shape[0]
    grid_spec = pltpu.PrefetchScalarGridSpec(
        num_scalar_prefetch=2,
        grid=(n_rows // MOE_BLK,),
        in_specs=[
            pl.BlockSpec((MOE_BLK, D), lambda i, be, nu: (i, 0)),
            pl.BlockSpec((1, D, 2 * D_FF), lambda i, be, nu: (be[i], 0, 0)),
            pl.BlockSpec((1, 1, 2 * D_FF), lambda i, be, nu: (be[i], 0, 0)),
            pl.BlockSpec((1, D_FF, D), lambda i, be, nu: (be[i], 0, 0)),
            pl.BlockSpec((1, 1, D), lambda i, be, nu: (be[i], 0, 0)),
        ],
        out_specs=pl.BlockSpec((MOE_BLK, D), lambda i, be, nu: (i, 0)),
    )
    return pl.pallas_call(
        _expert_kernel,
        grid_spec=grid_spec,
        out_shape=jax.ShapeDtypeStruct((n_rows, D), F32),
        compiler_params=_cparams(("arbitrary",)),
        name="moe_experts",
    )(block_e, n_used, xs, w1, b1, w2, b2)


def _combine_kernel(dest_ref, x_ref, gate_ref, m_ref, ys_ref, xo_ref, buf, sem):
    def body(r, carry):
        for k in range(TOP_K):
            src = dest_ref[0, 0, r * TOP_K + k]
            pltpu.make_async_copy(ys_ref.at[pl.ds(src, 1)], buf.at[k, pl.ds(r, 1)], sem).start()
        return carry

    lax.fori_loop(0, TM, body, 0)
    for k in range(TOP_K):
        pltpu.make_async_copy(buf.at[k], buf.at[k], sem).wait()
    gate = gate_ref[...]
    y = gate[:, 0:1] * buf[0]
    for k in range(1, TOP_K):
        y = y + gate[:, k:k + 1] * buf[k]
    xo_ref[...] = x_ref[...] + m_ref[0][:, 5 * D:6 * D] * y


def _combine(xn, gate, dest3, ys, mods3, mod_row):
    n = xn.shape[0]
    return pl.pallas_call(
        _combine_kernel,
        grid=(n // TM,),
        in_specs=[
            pl.BlockSpec((1, 1, TM * TOP_K), lambda i: (i, 0, 0), memory_space=pltpu.SMEM),
            pl.BlockSpec((TM, D), lambda i: (i, 0)),
            pl.BlockSpec((TM, LANES), lambda i: (i, 0)),
            pl.BlockSpec((1, 1, N_MOD * D), lambda i: (mod_row(i), 0, 0)),
            pl.BlockSpec(memory_space=pl.ANY),
        ],
        out_specs=pl.BlockSpec((TM, D), lambda i: (i, 0)),
        out_shape=jax.ShapeDtypeStruct((n, D), F32),
        scratch_shapes=[pltpu.VMEM((TOP_K, TM, D), F32), pltpu.SemaphoreType.DMA(())],
        compiler_params=_cparams(("arbitrary",)),
        name="moe_combine",
    )(dest3, xn, gate, mods3, ys)


def _final_norm_kernel(x_ref, g_ref, o_ref):
    o_ref[...] = _rms(x_ref[...], g_ref[...])


def _final_norm(xall, g, n_lat):
    return pl.pallas_call(
        _final_norm_kernel,
        grid=(n_lat // TM,),
        in_specs=[pl.BlockSpec((TM, D), lambda i: (i, 0)), pl.BlockSpec((1, D), lambda i: (0, 0))],
        out_specs=pl.BlockSpec((TM, D), lambda i: (i, 0)),
        out_shape=jax.ShapeDtypeStruct((n_lat, D), F32),
        compiler_params=_cparams(("arbitrary",)),
        name="final_norm",
    )(xall, g.reshape(1, D))


def kernel(x, c, ctx, c_ctx, mod_w, mod_b, norm1_g, norm2_g, w_in, s5_a_re, s5_a_im, s5_log_dt,
           s5_b_re, s5_b_im, s5_c_re, s5_c_im, s5_d, s5_w_glu, hgrn_lb_logits, hgrn_norm_g,
           w_up_s5, w_up_hgrn, w_out, router_w, router_b, moe_w1, moe_b1, moe_w2, moe_b2, final_g):
    nb, seq, d = x.shape
    nctx_tok = ctx.shape[1]
    depth = mod_w.shape[0]
    assert d == D and seq % (GRID_W * HG_C) == 0 and nctx_tok % TM == 0 and nb + 1 <= 8
    assert nctx_tok % (8 * S5_T) == 0 and seq % (8 * S5_T) == 0
    n_lat = nb * seq
    n = n_lat + nb * nctx_tok
    tiles_per_batch = seq // TM
    n_lat_tiles = n_lat // TM

    def mod_row(i):
        return jnp.where(i < n_lat_tiles, i // tiles_per_batch, nb)

    cpad = jnp.zeros((8, D), F32).at[:nb].set(c).at[nb].set(c_ctx)
    mods = _modulation(cpad, mod_w, mod_b)
    p_lb = jax.nn.softmax(hgrn_lb_logits.astype(F32), axis=1)
    lb_all = jnp.cumsum(p_lb, axis=1)[:, :depth]

    n_rows = pl.cdiv(n * TOP_K, MOE_BLK) * MOE_BLK + N_EXP * MOE_BLK
    n_blocks = n_rows // MOE_BLK

    xall = jnp.concatenate([x.reshape(n_lat, D), ctx.reshape(nb * nctx_tok, D)], axis=0)
    for l in range(depth):
        odd = l % 2 == 1
        mods3 = mods[l, :nb + 1].reshape(nb + 1, 1, N_MOD * D)
        z = _inproj(xall, mods3, norm1_g[l], w_in[l].astype(BF16), mod_row)
        mats = _s5_matrices(s5_a_re[l], s5_a_im[l], s5_log_dt[l], s5_b_re[l], s5_b_im[l],
                            s5_c_re[l], s5_c_im[l])
        y5s = _s5_branch(z, odd, nb, seq, nctx_tok, mats)
        hg_out = _hgrn_branch(z, odd, nb, seq, nctx_tok, lb_all[:, l])
        rw = jnp.zeros((D, LANES), F32).at[:, :N_EXP].set(router_w[l])
        rwh = rw.astype(BF16)
        rwl = (rw - rwh.astype(F32)).astype(BF16)
        rb = jnp.full((1, LANES), NEG, F32).at[0, :N_EXP].set(router_b[l])
        params = dict(d=s5_d[l].reshape(1, D_S5), wglu=s5_w_glu[l].astype(BF16),
                      hn=hgrn_norm_g[l].reshape(1, HG_DK), wus=w_up_s5[l].astype(BF16),
                      wuh=w_up_hgrn[l].astype(BF16), wo=w_out[l].astype(BF16),
                      n2=norm2_g[l].reshape(1, D), rwh=rwh, rwl=rwl, rb=rb)
        xn, h2, gate, meta, cnt = _post(xall, z, y5s, hg_out, mods3, mod_row, params, n_lat_tiles)
        counts = cnt[0, :N_EXP].astype(jnp.int32)
        padded = (counts + MOE_BLK - 1) // MOE_BLK * MOE_BLK
        pad_end = jnp.cumsum(padded)
        pad_start = pad_end - padded
        dest = pad_start[meta[:, :TOP_K]] + meta[:, TOP_K:2 * TOP_K]
        dest3 = dest.reshape(n // TM, 1, TM * TOP_K)
        block_e = jnp.minimum(
            jnp.searchsorted(pad_end, jnp.arange(n_blocks, dtype=jnp.int32) * MOE_BLK, side='right'),
            N_EXP - 1).astype(jnp.int32)
        n_used = (pad_end[-1:] // MOE_BLK).astype(jnp.int32)
        xs = _dispatch(h2, dest3, jnp.zeros((n_rows, D), F32))
        ys = _experts(block_e, n_used, xs, moe_w1[l].astype(BF16),
                      moe_b1[l].reshape(N_EXP, 1, 2 * D_FF), moe_w2[l].astype(BF16),
                      moe_b2[l].reshape(N_EXP, 1, D))
        xall = _combine(xn, gate, dest3, ys, mods3, mod_row)
    return _final_norm(xall, final_g, n_lat).reshape(nb, seq, D)
```

```python
import functools
import math

import jax
import jax.numpy as jnp
from jax import lax
from jax.experimental import pallas as pl
from jax.experimental.pallas import tpu as pltpu

F32 = jnp.float32
BF16 = jnp.bfloat16

D = 1024
GRID_W = 64
D_S5 = 512
S5_H = 16
S5_G = D_S5 // S5_H
S5_P = 64
D_HG = 512
HG_HEADS = 4
HG_DK = 128
N_IN = D_S5 + 5 * D_HG + 2 * D
N_EXP = 32
TOP_K = 4
D_FF = D
SWIGLU_LIMIT = 7.0
SWIGLU_ALPHA = 1.702
N_MOD = 6
EPS = 1e-6

LANES = 128
TM = 256
S5_T = 16
S5_GB = LANES // S5_H
S5_OCT = S5_G // S5_GB
S5_TL = S5_T * LANES
S5_SL = S5_GB * S5_P
HG_C = 64
HG_SUB = 16
HG_CLAMP = 80.0
MOE_BLK = 256
DMA_UNROLL = 4
VMEM_LIMIT = 56 * 1024 * 1024
NEG = -1e30


def _cparams(sem, vmem=VMEM_LIMIT):
    return pltpu.CompilerParams(dimension_semantics=sem, vmem_limit_bytes=vmem)


def _rms(x, g):
    return x * lax.rsqrt(jnp.mean(x * x, axis=-1, keepdims=True) + EPS) * g


def _split3(x):
    hi = x.astype(BF16)
    r1 = x - hi.astype(F32)
    mid = r1.astype(BF16)
    lo = (r1 - mid.astype(F32)).astype(BF16)
    return hi, mid, lo


def _mod_kernel(c_ref, w_ref, b_ref, o_ref):
    c = c_ref[...]
    sc = c * jax.nn.sigmoid(c)
    acc = jnp.zeros(o_ref.shape[1:], F32)
    w = w_ref[0]
    for piece in _split3(sc):
        for wp in _split3(w):
            acc = acc + jnp.dot(piece, wp, preferred_element_type=F32)
    o_ref[0] = acc + b_ref[0]


def _modulation(cpad, mod_w, mod_b):
    depth = mod_w.shape[0]
    tn = 1536
    return pl.pallas_call(
        _mod_kernel,
        grid=(depth, (N_MOD * D) // tn),
        in_specs=[
            pl.BlockSpec((8, D), lambda l, j: (0, 0)),
            pl.BlockSpec((1, D, tn), lambda l, j: (l, 0, j)),
            pl.BlockSpec((1, 1, tn), lambda l, j: (l, 0, j)),
        ],
        out_specs=pl.BlockSpec((1, 8, tn), lambda l, j: (l, 0, j)),
        out_shape=jax.ShapeDtypeStruct((depth, 8, N_MOD * D), F32),
        compiler_params=_cparams(("arbitrary", "arbitrary")),
        name="modulation",
    )(cpad, mod_w, mod_b.reshape(depth, 1, N_MOD * D))


def _inproj_kernel(x_ref, m_ref, g_ref, w_ref, z_ref):
    m = m_ref[0]
    h = _rms(x_ref[...], g_ref[...])
    h = h * (1.0 + m[:, D:2 * D]) + m[:, 0:D]
    z_ref[...] = jnp.dot(h.astype(BF16), w_ref[...], preferred_element_type=F32)


def _inproj(xall, mods3, g, w_bf, mod_row):
    n = xall.shape[0]
    return pl.pallas_call(
        _inproj_kernel,
        grid=(n // TM,),
        in_specs=[
            pl.BlockSpec((TM, D), lambda i: (i, 0)),
            pl.BlockSpec((1, 1, N_MOD * D), lambda i: (mod_row(i), 0, 0)),
            pl.BlockSpec((1, D), lambda i: (0, 0)),
            pl.BlockSpec((D, N_IN), lambda i: (0, 0)),
        ],
        out_specs=pl.BlockSpec((TM, N_IN), lambda i: (i, 0)),
        out_shape=jax.ShapeDtypeStruct((n, N_IN), F32),
        compiler_params=_cparams(("arbitrary",)),
        name="inproj",
    )(xall, mods3, g.reshape(1, D), w_bf)


def _s5_row_block(nc):
    return max(r for r in range(16, 257, 16) if nc % r == 0)


def _s5_kernel(u_ref, mcat_ref, v_ref, lam_ref, y_ref, d_scr, *, nc, nctx):
    rb = _s5_row_block(nc)
    for r0 in range(0, nc, rb):
        ub = u_ref[0, 0, r0:r0 + rb, :]
        y_ref[0, 0, r0:r0 + rb, :] = jnp.dot(ub, mcat_ref[0, :, :S5_TL],
                                              preferred_element_type=F32)
        d_scr[r0:r0 + rb, :] = jnp.dot(ub, mcat_ref[0, :, S5_TL:], preferred_element_type=F32)
    lam = lam_ref[0]
    sub = 8
    zero = jnp.zeros((sub, S5_SL), F32)
    srow = lax.broadcasted_iota(jnp.int32, (sub, S5_SL), 0)
    ntile, nctile = nc // sub, nctx // sub

    def bcast(row):
        return jnp.broadcast_to(row, (sub, S5_SL))

    def run(col, lr, li, tile_of, descending):
        def body(j, carry):
            xr, xi = carry
            base = pl.multiple_of(tile_of(j) * sub, sub)
            dt = d_scr[pl.ds(base, sub), col:col + 2 * S5_SL]
            accr, acci = zero, zero
            for step in range(sub):
                s = sub - 1 - step if descending else step
                accr = jnp.where(srow == s, xr, accr)
                acci = jnp.where(srow == s, xi, acci)
                dr = bcast(dt[s:s + 1, 0:S5_SL])
                di = bcast(dt[s:s + 1, S5_SL:2 * S5_SL])
                xr, xi = lr * xr - li * xi + dr, lr * xi + li * xr + di
            d_scr[pl.ds(base, sub), col:col + S5_SL] = accr
            d_scr[pl.ds(base, sub), col + S5_SL:col + 2 * S5_SL] = acci
            return xr, xi

        lax.fori_loop(0, ntile, body, (zero, zero))

    run(0, lam[0:1], lam[1:2], lambda j: j, False)
    run(2 * S5_SL, lam[2:3], lam[3:4],
        lambda j: jnp.where(j < nctile, nctile - 1 - j, ntile - 1 - (j - nctile)), True)
    for r0 in range(0, nc, rb):
        y_ref[0, 0, r0:r0 + rb, :] += jnp.dot(d_scr[r0:r0 + rb, :].astype(BF16), v_ref[0],
                                               preferred_element_type=F32)


def _s5_scan(u8, mcat, v8, lam, nb, nc, nctx):
    kern = functools.partial(_s5_kernel, nc=nc, nctx=nctx)
    once = pl.Buffered(1)
    return pl.pallas_call(
        kern,
        grid=(S5_OCT, nb),
        in_specs=[
            pl.BlockSpec((1, 1, nc, S5_TL), lambda o, b: (o, b, 0, 0)),
            pl.BlockSpec((1, S5_TL, S5_TL + 4 * S5_SL), lambda o, b: (o, 0, 0), pipeline_mode=once),
            pl.BlockSpec((1, 4 * S5_SL, S5_TL), lambda o, b: (o, 0, 0), pipeline_mode=once),
            pl.BlockSpec((1, 8, S5_SL), lambda o, b: (o, 0, 0)),
        ],
        out_specs=pl.BlockSpec((1, 1, nc, S5_TL), lambda o, b: (o, b, 0, 0), pipeline_mode=once),
        out_shape=jax.ShapeDtypeStruct((S5_OCT, nb, nc, S5_TL), F32),
        scratch_shapes=[pltpu.VMEM((nc, 4 * S5_SL), F32)],
        compiler_params=_cparams(("arbitrary", "arbitrary")),
        name="s5_scan",
    )(u8, mcat, v8, lam)


def _s5_matrices(a_re, a_im, log_dt, b_re, b_im, c_re, c_im):
    t = S5_T
    dt = jnp.exp(log_dt.astype(F32))[..., None]
    are, aim = a_re.astype(F32), a_im.astype(F32)

    def lam_pow(tau):
        tau = tau.astype(F32)[None, :, None, None]
        mag = jnp.exp(are[:, None] * dt[:, None] * tau)
        ang = aim[:, None] * dt[:, None] * tau
        return mag * jnp.cos(ang), mag * jnp.sin(ang)

    mag1 = jnp.exp(are * dt)
    lbr, lbi = mag1 * jnp.cos(aim * dt), mag1 * jnp.sin(aim * dt)
    den = are * are + aim * aim
    nr = lbr - 1.0
    qr = ((nr * are + lbi * aim) / den)[..., None]
    qi = ((lbi * are - nr * aim) / den)[..., None]
    bre, bim = b_re.astype(F32), b_im.astype(F32)
    bbr = qr * bre - qi * bim
    bbi = qr * bim + qi * bre
    cre, cim = c_re.astype(F32), c_im.astype(F32)

    pr, pi = lam_pow(jnp.arange(t + 1))
    cl_r = cre[:, None] * pr[:, :, :, None, :] - cim[:, None] * pi[:, :, :, None, :]
    cl_i = cre[:, None] * pi[:, :, :, None, :] + cim[:, None] * pr[:, :, :, None, :]
    kk = (jnp.einsum('dtgop,dgpi->dtgoi', cl_r[:, :t], bbr)
          - jnp.einsum('dtgop,dgpi->dtgoi', cl_i[:, :t], bbi))
    kf, kb = kk[0], kk[1]
    s_idx = jnp.arange(t)[:, None]
    t_idx = jnp.arange(t)[None, :]
    lag_f = t_idx - s_idx
    lag_b = s_idx - t_idx
    mf = jnp.where((lag_f >= 0)[:, :, None, None, None], kf[jnp.clip(lag_f, 0, t - 1)], 0.0)
    mb = jnp.where((lag_b >= 0)[:, :, None, None, None], kb[jnp.clip(lag_b, 0, t - 1)], 0.0)
    eye = jnp.eye(S5_GB, dtype=F32)
    m5 = (mf + mb).transpose(2, 0, 4, 1, 3).reshape(S5_OCT, S5_GB, t, S5_H, t, S5_H)
    m8 = jnp.einsum('ojsitu,jk->osjitku', m5, eye).reshape(S5_OCT, S5_TL, S5_TL)

    rev_r, rev_i = lam_pow(t - 1 - jnp.arange(t))
    pf_r, pf_i = rev_r[0], rev_i[0]
    pb_r, pb_i = pr[1, :t], pi[1, :t]

    def wmat(p_r, p_i, br, bi):
        wr = p_r[:, :, :, None] * br[None] - p_i[:, :, :, None] * bi[None]
        wi = p_r[:, :, :, None] * bi[None] + p_i[:, :, :, None] * br[None]

        def to(w):
            w5 = w.transpose(1, 0, 3, 2).reshape(S5_OCT, S5_GB, t, S5_H, S5_P)
            return jnp.einsum('ojshp,jk->osjhkp', w5, eye).reshape(S5_OCT, S5_TL, S5_SL)

        return to(wr), to(wi)

    wfr, wfi = wmat(pf_r, pf_i, bbr[0], bbi[0])
    wbr, wbi = wmat(pb_r, pb_i, bbr[1], bbi[1])
    mcat = jnp.concatenate([m8, wfr, wfi, wbr, wbi], axis=-1).astype(BF16)

    def vmat(clr, cli):
        def to(c):
            v5 = c.transpose(1, 3, 0, 2).reshape(S5_OCT, S5_GB, S5_P, t, S5_H)
            return jnp.einsum('ojpth,jk->ojptkh', v5, eye).reshape(S5_OCT, S5_SL, S5_TL)

        return jnp.concatenate([to(clr), to(-cli)], axis=1)

    vf = vmat(cl_r[0, 1:t + 1], cl_i[0, 1:t + 1])
    qr_, qi_ = lam_pow(t - jnp.arange(t))
    vb = vmat(cre[1][None] * qr_[1][:, :, None, :] - cim[1][None] * qi_[1][:, :, None, :],
              cre[1][None] * qi_[1][:, :, None, :] + cim[1][None] * qr_[1][:, :, None, :])
    v8 = jnp.concatenate([vf, vb], axis=1).astype(BF16)

    ltr, lti = pr[:, t], pi[:, t]
    rows = [ltr[0], lti[0], ltr[1], lti[1]]
    rows = [r.reshape(S5_OCT, S5_SL) for r in rows] + [jnp.zeros((S5_OCT, S5_SL), F32)] * 4
    lam = jnp.stack(rows, axis=1)
    return mcat, v8, lam


def _s5_branch(z, layer_odd, nb, seq, nctx_tok, mats):
    mcat, v8, lam = mats
    t = S5_T
    n_lat = nb * seq
    u_lat = z[:n_lat, :D_S5].astype(BF16)
    u_ctx = z[n_lat:, :D_S5].astype(BF16)
    rows = seq // GRID_W
    if layer_odd:
        ul = u_lat.reshape(nb, rows // t, t, GRID_W, S5_OCT, LANES).transpose(4, 0, 3, 1, 2, 5)
    else:
        ul = u_lat.reshape(nb, seq // t, t, S5_OCT, LANES).transpose(3, 0, 1, 2, 4)
    ul = ul.reshape(S5_OCT, nb, seq // t, S5_TL)
    uc = u_ctx.reshape(nb, nctx_tok // t, t, S5_OCT, LANES).transpose(3, 0, 1, 2, 4)
    uc = uc.reshape(S5_OCT, nb, nctx_tok // t, S5_TL)
    nctx = nctx_tok // t
    nc = nctx + seq // t
    u8 = jnp.concatenate([uc, ul], axis=2)
    y8 = _s5_scan(u8, mcat, v8, lam, nb, nc, nctx).reshape(S5_OCT, nb, nc, t, LANES)
    yc = y8[:, :, :nctx].transpose(1, 2, 3, 0, 4).reshape(nb * nctx_tok, D_S5)
    yl = y8[:, :, nctx:]
    if layer_odd:
        yl = yl.reshape(S5_OCT, nb, GRID_W, rows // t, t, LANES).transpose(1, 3, 4, 2, 0, 5)
    else:
        yl = yl.transpose(1, 2, 3, 0, 4)
    return jnp.concatenate([yl.reshape(n_lat, D_S5), yc], axis=0)


def _hgrn_dir(q_ref, v_ref, f_ref, lbv, o_ref, s_scr, b, d, backward):
    c = HG_C
    x = f_ref[...]
    f = lbv + (1.0 - lbv) * jax.nn.sigmoid(x)
    logf = jnp.log(f)
    kk = (1.0 - lbv) * jax.nn.sigmoid(-x)
    row = lax.broadcasted_iota(jnp.int32, (c, c), 0)
    col = lax.broadcasted_iota(jnp.int32, (c, c), 1)
    keep = (col >= row) if backward else (col <= row)
    tri = keep.astype(BF16)
    cum = jnp.zeros((c, D_HG), F32)
    for piece in _split3(logf):
        cum = cum + jnp.dot(tri, piece, preferred_element_type=F32)
    tot = cum[0:1] if backward else cum[c - 1:c]
    q = q_ref[...]
    v = v_ref[...]
    qe = q * jnp.exp(cum)
    kh = kk * jnp.exp(tot - cum)
    etot = jnp.exp(tot)
    nsub = c // HG_SUB
    for h in range(HG_HEADS):
        sl = slice(h * HG_DK, (h + 1) * HG_DK)
        qh, kh_h, vh, cumh = q[:, sl], kk[:, sl], v[:, sl], cum[:, sl]
        vb16 = vh.astype(BF16)
        st = s_scr[b, d, h]
        inter = lax.dot_general(qe[:, sl].astype(BF16), st.astype(BF16),
                                (((1,), (1,)), ((), ())), preferred_element_type=F32)
        blocks = []
        for i in range(nsub):
            lo, hi = i * HG_SUB, (i + 1) * HG_SUB
            if backward:
                ref = cumh[hi:hi + 1] if i < nsub - 1 else jnp.zeros((1, HG_DK), F32)
            else:
                ref = cumh[lo - 1:lo] if i > 0 else jnp.zeros((1, HG_DK), F32)
            qt = qh[lo:hi] * jnp.exp(cumh[lo:hi] - ref)
            kt = kh_h * jnp.exp(jnp.minimum(ref - cumh, HG_CLAMP))
            blocks.append(lax.dot_general(qt.astype(BF16), kt.astype(BF16),
                                          (((1,), (1,)), ((), ())), preferred_element_type=F32))
        scores = jnp.where(keep, jnp.concatenate(blocks, axis=0), 0.0)
        o_ref[b, :, sl] = inter + jnp.dot(scores.astype(BF16), vb16, preferred_element_type=F32)
        upd = lax.dot_general(vb16, kh[:, sl].astype(BF16),
                              (((0,), (0,)), ((), ())), preferred_element_type=F32)
        s_scr[b, d, h] = st * etot[:, sl] + upd


def _hgrn_kernel(*refs, nb):
    zin = refs[:6 * nb]
    lb_ref, s0_ref, of_ref, ob_ref, sfin_ref, s_scr = refs[6 * nb:]
    i = pl.program_id(0)

    @pl.when(i == 0)
    def _():
        s_scr[...] = s0_ref[...]

    for b in range(nb):
        qf, vf, ff, qb, vb, fb = zin[6 * b:6 * b + 6]
        _hgrn_dir(qf, vf, ff, lb_ref[0:1], of_ref, s_scr, b, 0, False)
        _hgrn_dir(qb, vb, fb, lb_ref[1:2], ob_ref, s_scr, b, 1, True)

    @pl.when(i == pl.num_programs(0) - 1)
    def _():
        sfin_ref[...] = s_scr[...]


def _hgrn_scan(zview, lb2, s0, nb, nsteps, in_f, in_b, out_f, out_b, out_view_shape):
    blk = (HG_C, D_HG)
    oblk = (nb, HG_C, D_HG)
    state_shape = (nb, 2, HG_HEADS, HG_DK, HG_DK)
    state_spec = pl.BlockSpec(state_shape, lambda i: (0, 0, 0, 0, 0))
    in_specs = []
    for b in range(nb):
        in_specs += [pl.BlockSpec(blk, lambda i, b=b, cb=cb: in_f(b, i, cb)) for cb in (1, 2, 3)]
        in_specs += [pl.BlockSpec(blk, lambda i, b=b, cb=cb: in_b(b, i, cb)) for cb in (1, 2, 4)]
    in_specs += [pl.BlockSpec((2, D_HG), lambda i: (0, 0)), state_spec]
    return pl.pallas_call(
        functools.partial(_hgrn_kernel, nb=nb),
        grid=(nsteps,),
        in_specs=in_specs,
        out_specs=[pl.BlockSpec(oblk, lambda i: (0,) + out_f(i)),
                   pl.BlockSpec(oblk, lambda i: (0,) + out_b(i)), state_spec],
        out_shape=[jax.ShapeDtypeStruct(out_view_shape, F32),
                   jax.ShapeDtypeStruct(out_view_shape, F32),
                   jax.ShapeDtypeStruct(state_shape, F32)],
        scratch_shapes=[pltpu.VMEM(state_shape, F32)],
        compiler_params=_cparams(("arbitrary",)),
        name="hgrn_scan",
    )(*([zview] * (6 * nb) + [lb2, s0]))


def _hgrn_branch(z, layer_odd, nb, seq, nctx_tok, lb2):
    n = z.shape[0]
    n_lat = nb * seq
    c = HG_C
    ncols = N_IN // D_HG
    nsc = nctx_tok // c
    base_c = n_lat // c
    s0 = jnp.zeros((nb, 2, HG_HEADS, HG_DK, HG_DK), F32)
    oc_f, oc_b, s_ctx = _hgrn_scan(
        z, lb2, s0, nb, nsc,
        lambda b, i, cb: (base_c + b * nsc + i, cb),
        lambda b, i, cb: (base_c + b * nsc + (nsc - 1 - i), cb),
        lambda i: (i, 0),
        lambda i: (nsc - 1 - i, 0),
        (nb, nctx_tok, D_HG))
    nsl = seq // c
    if layer_odd:
        rpc = seq // GRID_W // c
        zview = z.reshape(n // GRID_W, GRID_W * N_IN)
        oview = (nb, seq // GRID_W, GRID_W * D_HG)

        def lat_in(b, i, cb):
            return (b * rpc + i % rpc, (i // rpc) * ncols + cb)

        def lat_out(i):
            return (i % rpc, i // rpc)
    else:
        zview = z
        oview = (nb, seq, D_HG)

        def lat_in(b, i, cb):
            return (b * nsl + i, cb)

        def lat_out(i):
            return (i, 0)

    ol_f, ol_b, _ = _hgrn_scan(
        zview, lb2, s_ctx, nb, nsl,
        lat_in, lambda b, i, cb: lat_in(b, nsl - 1 - i, cb),
        lat_out, lambda i: lat_out(nsl - 1 - i), oview)
    return (ol_f.reshape(n_lat, D_HG), ol_b.reshape(n_lat, D_HG),
            oc_f.reshape(nb * nctx_tok, D_HG), oc_b.reshape(nb * nctx_tok, D_HG))


def _gelu_tanh(x):
    return 0.5 * x * (1.0 + jnp.tanh(math.sqrt(2.0 / math.pi) * (x + 0.044715 * (x * x * x))))


def _post_kernel(x_ref, u_ref, g_ref, gs_ref, gh_ref, y5_ref, ofl_ref, obl_ref, ofc_ref, obc_ref,
                 m_ref, d_ref, wglu_ref, hn_ref, wus_ref, wuh_ref, wo_ref, n2_ref,
                 rwh_ref, rwl_ref, rb_ref,
                 xo_ref, h2_ref, gate_ref, meta_ref, cnt_ref, run_scr, *, n_lat_tiles):
    i = pl.program_id(0)

    @pl.when(i == 0)
    def _():
        run_scr[...] = jnp.zeros_like(run_scr)

    m = m_ref[0]
    zz = _gelu_tanh(y5_ref[...] + d_ref[...] * u_ref[...])
    glu = jnp.dot(zz.astype(BF16), wglu_ref[...], preferred_element_type=F32)
    y5 = zz * jax.nn.sigmoid(glu)
    o = jnp.where(i < n_lat_tiles, ofl_ref[...] + obl_ref[...], ofc_ref[...] + obc_ref[...])
    hn = hn_ref[...]
    on = jnp.concatenate(
        [_rms(o[:, h * HG_DK:(h + 1) * HG_DK], hn) for h in range(HG_HEADS)], axis=1)
    g = g_ref[...]
    yh = on * (g * jax.nn.sigmoid(g))
    merged = (jax.nn.sigmoid(gs_ref[...])
              * jnp.dot(y5.astype(BF16), wus_ref[...], preferred_element_type=F32)
              + jax.nn.sigmoid(gh_ref[...])
              * jnp.dot(yh.astype(BF16), wuh_ref[...], preferred_element_type=F32))
    y = jnp.dot(merged.astype(BF16), wo_ref[...], preferred_element_type=F32)
    xn = x_ref[...] + m[:, 2 * D:3 * D] * y
    xo_ref[...] = xn
    h2 = _rms(xn, n2_ref[...]) * (1.0 + m[:, 4 * D:5 * D]) + m[:, 3 * D:4 * D]
    h2_ref[...] = h2

    hi = h2.astype(BF16)
    lo = (h2 - hi.astype(F32)).astype(BF16)
    rwh = rwh_ref[...]
    logits = (jnp.dot(hi, rwh, preferred_element_type=F32)
              + jnp.dot(lo, rwh, preferred_element_type=F32)
              + jnp.dot(hi, rwl_ref[...], preferred_element_type=F32)) + rb_ref[...]
    tm = logits.shape[0]
    lane = lax.broadcasted_iota(jnp.int32, (tm, LANES), 1)
    lane_f = lane.astype(F32)
    l = logits
    vals, idxs, hots = [], [], []
    for _ in range(TOP_K):
        mk = jnp.max(l, axis=1, keepdims=True)
        ik = jnp.min(jnp.where(l == mk, lane_f, float(LANES)), axis=1, keepdims=True)
        hot = lane_f == ik
        vals.append(mk)
        idxs.append(ik)
        hots.append(hot)
        l = jnp.where(hot, NEG * 10.0, l)
    exps = [jnp.exp(vk - vals[0]) for vk in vals]
    den = exps[0] + exps[1] + exps[2] + exps[3]
    sel = (hots[0] | hots[1] | hots[2] | hots[3]).astype(F32)
    r_i = lax.broadcasted_iota(jnp.int32, (tm, tm), 0)
    c_i = lax.broadcasted_iota(jnp.int32, (tm, tm), 1)
    strict = (c_i < r_i).astype(BF16)
    pos = jnp.dot(strict, sel.astype(BF16), preferred_element_type=F32) + run_scr[...]
    run_scr[...] += jnp.sum(sel, axis=0, keepdims=True)
    cnt_ref[...] = run_scr[...]
    gate = jnp.zeros((tm, LANES), F32)
    meta = jnp.zeros((tm, LANES), jnp.int32)
    for k in range(TOP_K):
        pk = jnp.sum(jnp.where(hots[k], pos, 0.0), axis=1, keepdims=True).astype(jnp.int32)
        gate = jnp.where(lane == k, exps[k] / den, gate)
        meta = jnp.where(lane == k, idxs[k].astype(jnp.int32), meta)
        meta = jnp.where(lane == TOP_K + k, pk, meta)
    gate_ref[...] = gate
    meta_ref[...] = meta


def _post(xall, z, y5s, hg_out, mods3, mod_row, p, n_lat_tiles):
    n = xall.shape[0]
    tile = lambda w, cb: pl.BlockSpec((TM, w), lambda i: (i, cb))
    full = lambda a: pl.BlockSpec(a.shape, lambda i: (0,) * a.ndim)
    lat_tile = pl.BlockSpec((TM, D_HG), lambda i: (jnp.minimum(i, n_lat_tiles - 1), 0))
    ctx_tile = pl.BlockSpec((TM, D_HG), lambda i: (jnp.maximum(i - n_lat_tiles, 0), 0))
    consts = [p['d'], p['wglu'], p['hn'], p['wus'], p['wuh'], p['wo'], p['n2'],
              p['rwh'], p['rwl'], p['rb']]
    return pl.pallas_call(
        functools.partial(_post_kernel, n_lat_tiles=n_lat_tiles),
        grid=(n // TM,),
        in_specs=[
            tile(D, 0),
            tile(D_S5, 0),
            tile(D_HG, 5),
            tile(D, 3),
            tile(D, 4),
            tile(D_S5, 0), lat_tile, lat_tile, ctx_tile, ctx_tile,
            pl.BlockSpec((1, 1, N_MOD * D), lambda i: (mod_row(i), 0, 0)),
        ] + [full(a) for a in consts],
        out_specs=[tile(D, 0), tile(D, 0), tile(LANES, 0), tile(LANES, 0),
                   pl.BlockSpec((1, LANES), lambda i: (0, 0))],
        out_shape=[jax.ShapeDtypeStruct((n, D), F32), jax.ShapeDtypeStruct((n, D), F32),
                   jax.ShapeDtypeStruct((n, LANES), F32), jax.ShapeDtypeStruct((n, LANES), jnp.int32),
                   jax.ShapeDtypeStruct((1, LANES), F32)],
        scratch_shapes=[pltpu.VMEM((1, LANES), F32)],
        compiler_params=_cparams(("arbitrary",)),
        name="mixer_post",
    )(xall, z, z, z, z, y5s, *hg_out, mods3, *consts)


def _dispatch_kernel(dest_ref, h_ref, xs_in, xs_ref, sem):
    del xs_in

    def body(r, carry):
        for k in range(TOP_K):
            dst = dest_ref[0, 0, r * TOP_K + k]
            pltpu.make_async_copy(h_ref.at[pl.ds(r, 1)], xs_ref.at[pl.ds(dst, 1)],
                                  sem).start(priority=k % 2)
        return carry

    lax.fori_loop(0, TM, body, 0, unroll=DMA_UNROLL)
    for _ in range(TOP_K):
        pltpu.make_async_copy(h_ref, h_ref, sem).wait()


def _dispatch(h2, dest3, xs0):
    n = h2.shape[0]
    return pl.pallas_call(
        _dispatch_kernel,
        grid=(n // TM,),
        in_specs=[
            pl.BlockSpec((1, 1, TM * TOP_K), lambda i: (i, 0, 0), memory_space=pltpu.SMEM),
            pl.BlockSpec((TM, D), lambda i: (i, 0)),
            pl.BlockSpec(memory_space=pl.ANY),
        ],
        out_specs=pl.BlockSpec(memory_space=pl.ANY),
        out_shape=jax.ShapeDtypeStruct(xs0.shape, F32),
        scratch_shapes=[pltpu.SemaphoreType.DMA(())],
        input_output_aliases={2: 0},
        compiler_params=pltpu.CompilerParams(dimension_semantics=("arbitrary",),
                                             vmem_limit_bytes=VMEM_LIMIT, has_side_effects=True),
        name="moe_dispatch",
    )(dest3, h2, xs0)


def _expert_kernel(be_ref, nu_ref, xs_ref, w1_ref, b1_ref, w2_ref, b2_ref, ys_ref, w1_scr, w2_scr):
    i = pl.program_id(0)
    prev = be_ref[jnp.maximum(i - 1, 0)]

    @pl.when((i == 0) | (be_ref[i] != prev))
    def _():
        rows = 256
        for r in range(0, D, rows):
            w1_scr[r:r + rows, :] = w1_ref[0, r:r + rows, :].astype(BF16)
        for r in range(0, D_FF, rows):
            w2_scr[r:r + rows, :] = w2_ref[0, r:r + rows, :].astype(BF16)

    @pl.when(i < nu_ref[0])
    def _():
        gu = jnp.dot(xs_ref[...].astype(BF16), w1_scr[...], preferred_element_type=F32) + b1_ref[0]
        g = jnp.minimum(gu[:, :D_FF], SWIGLU_LIMIT)
        u = jnp.clip(gu[:, D_FF:], -SWIGLU_LIMIT, SWIGLU_LIMIT)
        a = g * jax.nn.sigmoid(SWIGLU_ALPHA * g) * (u + 1.0)
        ys_ref[...] = jnp.dot(a.astype(BF16), w2_scr[...], preferred_element_type=F32) + b2_ref[0]

    @pl.when(i >= nu_ref[0])
    def _():
        ys_ref[...] = jnp.zeros_like(ys_ref)


def _experts(block_e, n_used, xs, w1, b1, w2, b2):
    n_rows = xs.shape[0]
    grid_spec = pltpu.PrefetchScalarGridSpec(
        num_scalar_prefetch=2,
        grid=(n_rows // MOE_BLK,),
        in_specs=[
            pl.BlockSpec((MOE_BLK, D), lambda i, be, nu: (i, 0)),
            pl.BlockSpec((1, D, 2 * D_FF), lambda i, be, nu: (be[i], 0, 0)),
            pl.BlockSpec((1, 1, 2 * D_FF), lambda i, be, nu: (be[i], 0, 0)),
            pl.BlockSpec((1, D_FF, D), lambda i, be, nu: (be[i], 0, 0)),
            pl.BlockSpec((1, 1, D), lambda i, be, nu: (be[i], 0, 0)),
        ],
        out_specs=pl.BlockSpec((MOE_BLK, D), lambda i, be, nu: (i, 0)),
        scratch_shapes=[pltpu.VMEM((D, 2 * D_FF), BF16), pltpu.VMEM((D_FF, D), BF16)],
    )
    return pl.pallas_call(
        _expert_kernel,
        grid_spec=grid_spec,
        out_shape=jax.ShapeDtypeStruct((n_rows, D), F32),
        compiler_params=_cparams(("arbitrary",)),
        name="moe_experts",
    )(block_e, n_used, xs, w1, b1, w2, b2)


def _combine_kernel(dest_ref, x_ref, gate_ref, m_ref, ys_ref, xo_ref, buf, sem):
    def body(r, carry):
        for k in range(TOP_K):
            src = dest_ref[0, 0, r * TOP_K + k]
            pltpu.make_async_copy(ys_ref.at[pl.ds(src, 1)], buf.at[k, pl.ds(r, 1)],
                                  sem).start(priority=k % 2)
        return carry

    lax.fori_loop(0, TM, body, 0, unroll=DMA_UNROLL)
    for k in range(TOP_K):
        pltpu.make_async_copy(buf.at[k], buf.at[k], sem).wait()
    gate = gate_ref[...]
    y = gate[:, 0:1] * buf[0]
    for k in range(1, TOP_K):
        y = y + gate[:, k:k + 1] * buf[k]
    xo_ref[...] = x_ref[...] + m_ref[0][:, 5 * D:6 * D] * y


def _combine(xn, gate, dest3, ys, mods3, mod_row):
    n = xn.shape[0]
    return pl.pallas_call(
        _combine_kernel,
        grid=(n // TM,),
        in_specs=[
            pl.BlockSpec((1, 1, TM * TOP_K), lambda i: (i, 0, 0), memory_space=pltpu.SMEM),
            pl.BlockSpec((TM, D), lambda i: (i, 0)),
            pl.BlockSpec((TM, LANES), lambda i: (i, 0)),
            pl.BlockSpec((1, 1, N_MOD * D), lambda i: (mod_row(i), 0, 0)),
            pl.BlockSpec(memory_space=pl.ANY),
        ],
        out_specs=pl.BlockSpec((TM, D), lambda i: (i, 0)),
        out_shape=jax.ShapeDtypeStruct((n, D), F32),
        scratch_shapes=[pltpu.VMEM((TOP_K, TM, D), F32), pltpu.SemaphoreType.DMA(())],
        compiler_params=_cparams(("arbitrary",)),
        name="moe_combine",
    )(dest3, xn, gate, mods3, ys)


def _final_norm_kernel(x_ref, g_ref, o_ref):
    o_ref[...] = _rms(x_ref[...], g_ref[...])


def _final_norm(xall, g, n_lat):
    return pl.pallas_call(
        _final_norm_kernel,
        grid=(n_lat // TM,),
        in_specs=[pl.BlockSpec((TM, D), lambda i: (i, 0)), pl.BlockSpec((1, D), lambda i: (0, 0))],
        out_specs=pl.BlockSpec((TM, D), lambda i: (i, 0)),
        out_shape=jax.ShapeDtypeStruct((n_lat, D), F32),
        compiler_params=_cparams(("arbitrary",)),
        name="final_norm",
    )(xall, g.reshape(1, D))


def kernel(x, c, ctx, c_ctx, mod_w, mod_b, norm1_g, norm2_g, w_in, s5_a_re, s5_a_im, s5_log_dt,
           s5_b_re, s5_b_im, s5_c_re, s5_c_im, s5_d, s5_w_glu, hgrn_lb_logits, hgrn_norm_g,
           w_up_s5, w_up_hgrn, w_out, router_w, router_b, moe_w1, moe_b1, moe_w2, moe_b2, final_g):
    nb, seq, d = x.shape
    nctx_tok = ctx.shape[1]
    depth = mod_w.shape[0]
    assert d == D and seq % (GRID_W * HG_C) == 0 and nctx_tok % TM == 0 and nb + 1 <= 8
    assert nctx_tok % (8 * S5_T) == 0 and seq % (8 * S5_T) == 0
    n_lat = nb * seq
    n = n_lat + nb * nctx_tok
    tiles_per_batch = seq // TM
    n_lat_tiles = n_lat // TM

    def mod_row(i):
        return jnp.where(i < n_lat_tiles, i // tiles_per_batch, nb)

    cpad = jnp.zeros((8, D), F32).at[:nb].set(c).at[nb].set(c_ctx)
    mods = _modulation(cpad, mod_w, mod_b)
    p_lb = jax.nn.softmax(hgrn_lb_logits.astype(F32), axis=1)
    lb_all = jnp.cumsum(p_lb, axis=1)[:, :depth]

    n_rows = pl.cdiv(n * TOP_K, MOE_BLK) * MOE_BLK + N_EXP * MOE_BLK
    n_blocks = n_rows // MOE_BLK

    xall = jnp.concatenate([x.reshape(n_lat, D), ctx.reshape(nb * nctx_tok, D)], axis=0)
    for l in range(depth):
        odd = l % 2 == 1
        mods3 = mods[l, :nb + 1].reshape(nb + 1, 1, N_MOD * D)
        z = _inproj(xall, mods3, norm1_g[l], w_in[l].astype(BF16), mod_row)
        mats = _s5_matrices(s5_a_re[l], s5_a_im[l], s5_log_dt[l], s5_b_re[l], s5_b_im[l],
                            s5_c_re[l], s5_c_im[l])
        y5s = _s5_branch(z, odd, nb, seq, nctx_tok, mats)
        hg_out = _hgrn_branch(z, odd, nb, seq, nctx_tok, lb_all[:, l])
        rw = jnp.zeros((D, LANES), F32).at[:, :N_EXP].set(router_w[l])
        rwh = rw.astype(BF16)
        rwl = (rw - rwh.astype(F32)).astype(BF16)
        rb = jnp.full((1, LANES), NEG, F32).at[0, :N_EXP].set(router_b[l])
        params = dict(d=s5_d[l].reshape(1, D_S5), wglu=s5_w_glu[l].astype(BF16),
                      hn=hgrn_norm_g[l].reshape(1, HG_DK), wus=w_up_s5[l].astype(BF16),
                      wuh=w_up_hgrn[l].astype(BF16), wo=w_out[l].astype(BF16),
                      n2=norm2_g[l].reshape(1, D), rwh=rwh, rwl=rwl, rb=rb)
        xn, h2, gate, meta, cnt = _post(xall, z, y5s, hg_out, mods3, mod_row, params, n_lat_tiles)
        counts = cnt[0, :N_EXP].astype(jnp.int32)
        padded = (counts + MOE_BLK - 1) // MOE_BLK * MOE_BLK
        pad_end = jnp.cumsum(padded)
        pad_start = pad_end - padded
        dest = pad_start[meta[:, :TOP_K]] + meta[:, TOP_K:2 * TOP_K]
        dest3 = dest.reshape(n // TM, 1, TM * TOP_K)
        blk_start = jnp.arange(n_blocks, dtype=jnp.int32) * MOE_BLK
        block_e = jnp.minimum(
            jnp.sum((pad_end[None, :] <= blk_start[:, None]).astype(jnp.int32), axis=1), N_EXP - 1)
        n_used = (pad_end[-1:] // MOE_BLK).astype(jnp.int32)
        xs = _dispatch(h2, dest3, jnp.zeros((n_rows, D), F32))
        ys = _experts(block_e, n_used, xs, moe_w1[l], moe_b1[l].reshape(N_EXP, 1, 2 * D_FF),
                      moe_w2[l], moe_b2[l].reshape(N_EXP, 1, D))
        xall = _combine(xn, gate, dest3, ys, mods3, mod_row)
    return _final_norm(xall, final_g, n_lat).reshape(nb, seq, D)
```

```python
import functools
import math

import jax
import jax.numpy as jnp
from jax import lax
from jax.experimental import pallas as pl
from jax.experimental.pallas import tpu as pltpu

F32 = jnp.float32
BF16 = jnp.bfloat16

D = 1024
GRID_W = 64
D_S5 = 512
S5_H = 16
S5_G = D_S5 // S5_H
S5_P = 64
D_HG = 512
HG_HEADS = 4
HG_DK = 128
N_IN = D_S5 + 5 * D_HG + 2 * D
N_EXP = 32
TOP_K = 4
D_FF = D
SWIGLU_LIMIT = 7.0
SWIGLU_ALPHA = 1.702
N_MOD = 6
EPS = 1e-6

LANES = 128
TM = 256
S5_T = 16
S5_GB = LANES // S5_H
S5_OCT = S5_G // S5_GB
S5_TL = S5_T * LANES
S5_SL = S5_GB * S5_P
HG_C = 64
HG_SUB = 16
HG_CLAMP = 80.0
MOE_BLK = 256
DMA_UNROLL = 4
VMEM_LIMIT = 56 * 1024 * 1024
NEG = -1e30


def _cparams(sem, vmem=VMEM_LIMIT):
    return pltpu.CompilerParams(dimension_semantics=sem, vmem_limit_bytes=vmem)


def _rms(x, g):
    return x * lax.rsqrt(jnp.mean(x * x, axis=-1, keepdims=True) + EPS) * g


def _split3(x):
    hi = x.astype(BF16)
    r1 = x - hi.astype(F32)
    mid = r1.astype(BF16)
    lo = (r1 - mid.astype(F32)).astype(BF16)
    return hi, mid, lo


def _mod_kernel(c_ref, w_ref, b_ref, o_ref):
    c = c_ref[...]
    sc = c * jax.nn.sigmoid(c)
    acc = jnp.zeros(o_ref.shape[1:], F32)
    w = w_ref[0]
    for piece in _split3(sc):
        for wp in _split3(w):
            acc = acc + jnp.dot(piece, wp, preferred_element_type=F32)
    o_ref[0] = acc + b_ref[0]


def _modulation(cpad, mod_w, mod_b):
    depth = mod_w.shape[0]
    tn = 1536
    return pl.pallas_call(
        _mod_kernel,
        grid=(depth, (N_MOD * D) // tn),
        in_specs=[
            pl.BlockSpec((8, D), lambda l, j: (0, 0)),
            pl.BlockSpec((1, D, tn), lambda l, j: (l, 0, j)),
            pl.BlockSpec((1, 1, tn), lambda l, j: (l, 0, j)),
        ],
        out_specs=pl.BlockSpec((1, 8, tn), lambda l, j: (l, 0, j)),
        out_shape=jax.ShapeDtypeStruct((depth, 8, N_MOD * D), F32),
        compiler_params=_cparams(("arbitrary", "arbitrary")),
        name="modulation",
    )(cpad, mod_w, mod_b.reshape(depth, 1, N_MOD * D))


def _inproj_kernel(x_ref, m_ref, g_ref, w_ref, z_ref, u_ref):
    m = m_ref[0]
    h = _rms(x_ref[...], g_ref[...])
    h = h * (1.0 + m[:, D:2 * D]) + m[:, 0:D]
    z = jnp.dot(h.astype(BF16), w_ref[...], preferred_element_type=F32)
    z_ref[...] = z
    u_ref[...] = z[:, :D_S5].astype(BF16)


def _inproj(xall, mods3, g, w_bf, mod_row):
    n = xall.shape[0]
    return pl.pallas_call(
        _inproj_kernel,
        grid=(n // TM,),
        in_specs=[
            pl.BlockSpec((TM, D), lambda i: (i, 0)),
            pl.BlockSpec((1, 1, N_MOD * D), lambda i: (mod_row(i), 0, 0)),
            pl.BlockSpec((1, D), lambda i: (0, 0)),
            pl.BlockSpec((D, N_IN), lambda i: (0, 0)),
        ],
        out_specs=[pl.BlockSpec((TM, N_IN), lambda i: (i, 0)),
                   pl.BlockSpec((TM, D_S5), lambda i: (i, 0))],
        out_shape=[jax.ShapeDtypeStruct((n, N_IN), F32), jax.ShapeDtypeStruct((n, D_S5), BF16)],
        compiler_params=_cparams(("arbitrary",)),
        name="inproj",
    )(xall, mods3, g.reshape(1, D), w_bf)


def _s5_row_block(nc):
    return max(r for r in range(16, 257, 16) if nc % r == 0)


def _s5_expand(kd_ref, wd_ref, vd_ref, mcat_scr, v_scr):
    t = S5_T

    def rep(n_in, n_out):
        r = lax.broadcasted_iota(jnp.int32, (n_in, n_out), 0)
        c = lax.broadcasted_iota(jnp.int32, (n_in, n_out), 1)
        return (c % n_in == r).astype(BF16)

    def group_mask(rows, cols, rdiv, cdiv):
        r = lax.broadcasted_iota(jnp.int32, (rows, cols), 0)
        c = lax.broadcasted_iota(jnp.int32, (rows, cols), 1)
        return r // rdiv == c // cdiv

    def widen(compact, r01, mask):
        full = lax.dot_general(compact.astype(BF16), r01, (((0,), (0,)), ((), ())),
                               preferred_element_type=F32)
        return jnp.where(mask, full, 0.0).astype(BF16)

    r_h, r_p = rep(S5_H, LANES), rep(S5_P, S5_SL)
    m_mask = group_mask(LANES, LANES, S5_H, S5_H)
    w_mask = group_mask(LANES, S5_SL, S5_H, S5_P)
    v_mask = group_mask(S5_SL, LANES, S5_P, S5_H)
    for lag in range(2 * t - 1):
        tile = widen(kd_ref[0, lag], r_h, m_mask)
        for s in range(t):
            tt = s + lag - (t - 1)
            if 0 <= tt < t:
                mcat_scr[s * LANES:(s + 1) * LANES, tt * LANES:(tt + 1) * LANES] = tile
    for kind in range(4):
        for s in range(t):
            c0 = S5_TL + kind * S5_SL
            mcat_scr[s * LANES:(s + 1) * LANES, c0:c0 + S5_SL] = widen(
                wd_ref[0, kind * t + s], r_p, w_mask)
            v_scr[kind * S5_SL:(kind + 1) * S5_SL, s * LANES:(s + 1) * LANES] = widen(
                vd_ref[0, kind * t + s], r_h, v_mask)


def _s5_kernel(u_ref, kd_ref, wd_ref, vd_ref, lam_ref, y_ref, d_scr, mcat_scr, v_scr, *, nc, nctx):
    @pl.when(pl.program_id(1) == 0)
    def _():
        _s5_expand(kd_ref, wd_ref, vd_ref, mcat_scr, v_scr)

    rb = _s5_row_block(nc)
    for r0 in range(0, nc, rb):
        ub = u_ref[0, 0, r0:r0 + rb, :]
        y_ref[0, 0, r0:r0 + rb, :] = jnp.dot(ub, mcat_scr[:, :S5_TL], preferred_element_type=F32)
        d_scr[r0:r0 + rb, :] = jnp.dot(ub, mcat_scr[:, S5_TL:], preferred_element_type=F32)
    lam = lam_ref[0]
    sub = 8
    zero = jnp.zeros((sub, S5_SL), F32)
    srow = lax.broadcasted_iota(jnp.int32, (sub, S5_SL), 0)
    ntile, nctile = nc // sub, nctx // sub

    def bcast(row):
        return jnp.broadcast_to(row, (sub, S5_SL))

    def run(col, lr, li, tile_of, descending):
        def body(j, carry):
            xr, xi = carry
            base = pl.multiple_of(tile_of(j) * sub, sub)
            dt = d_scr[pl.ds(base, sub), col:col + 2 * S5_SL]
            accr, acci = zero, zero
            for step in range(sub):
                s = sub - 1 - step if descending else step
                accr = jnp.where(srow == s, xr, accr)
                acci = jnp.where(srow == s, xi, acci)
                dr = bcast(dt[s:s + 1, 0:S5_SL])
                di = bcast(dt[s:s + 1, S5_SL:2 * S5_SL])
                xr, xi = lr * xr - li * xi + dr, lr * xi + li * xr + di
            d_scr[pl.ds(base, sub), col:col + S5_SL] = accr
            d_scr[pl.ds(base, sub), col + S5_SL:col + 2 * S5_SL] = acci
            return xr, xi

        lax.fori_loop(0, ntile, body, (zero, zero))

    run(0, lam[0:1], lam[1:2], lambda j: j, False)
    run(2 * S5_SL, lam[2:3], lam[3:4],
        lambda j: jnp.where(j < nctile, nctile - 1 - j, ntile - 1 - (j - nctile)), True)
    for r0 in range(0, nc, rb):
        y_ref[0, 0, r0:r0 + rb, :] += jnp.dot(d_scr[r0:r0 + rb, :].astype(BF16), v_scr[...],
                                               preferred_element_type=F32)


def _s5_scan(u8, kd, wd, vd, lam, nb, nc, nctx):
    kern = functools.partial(_s5_kernel, nc=nc, nctx=nctx)
    once = pl.Buffered(1)
    t = S5_T
    return pl.pallas_call(
        kern,
        grid=(S5_OCT, nb),
        in_specs=[
            pl.BlockSpec((1, 1, nc, S5_TL), lambda o, b: (o, b, 0, 0)),
            pl.BlockSpec((1, 2 * t - 1, S5_H, LANES), lambda o, b: (o, 0, 0, 0), pipeline_mode=once),
            pl.BlockSpec((1, 4 * t, S5_P, LANES), lambda o, b: (o, 0, 0, 0), pipeline_mode=once),
            pl.BlockSpec((1, 4 * t, S5_H, S5_SL), lambda o, b: (o, 0, 0, 0), pipeline_mode=once),
            pl.BlockSpec((1, 8, S5_SL), lambda o, b: (o, 0, 0)),
        ],
        out_specs=pl.BlockSpec((1, 1, nc, S5_TL), lambda o, b: (o, b, 0, 0), pipeline_mode=once),
        out_shape=jax.ShapeDtypeStruct((S5_OCT, nb, nc, S5_TL), F32),
        scratch_shapes=[pltpu.VMEM((nc, 4 * S5_SL), F32),
                        pltpu.VMEM((S5_TL, S5_TL + 4 * S5_SL), BF16),
                        pltpu.VMEM((4 * S5_SL, S5_TL), BF16)],
        compiler_params=_cparams(("arbitrary", "arbitrary")),
        name="s5_scan",
    )(u8, kd, wd, vd, lam)


def _s5_matrices(a_re, a_im, log_dt, b_re, b_im, c_re, c_im):
    t = S5_T
    dt = jnp.exp(log_dt.astype(F32))[..., None]
    are, aim = a_re.astype(F32), a_im.astype(F32)

    def lam_pow(tau):
        tau = tau.astype(F32)[None, :, None, None]
        mag = jnp.exp(are[:, None] * dt[:, None] * tau)
        ang = aim[:, None] * dt[:, None] * tau
        return mag * jnp.cos(ang), mag * jnp.sin(ang)

    mag1 = jnp.exp(are * dt)
    lbr, lbi = mag1 * jnp.cos(aim * dt), mag1 * jnp.sin(aim * dt)
    den = are * are + aim * aim
    nr = lbr - 1.0
    qr = ((nr * are + lbi * aim) / den)[..., None]
    qi = ((lbi * are - nr * aim) / den)[..., None]
    bre, bim = b_re.astype(F32), b_im.astype(F32)
    bbr = qr * bre - qi * bim
    bbi = qr * bim + qi * bre
    cre, cim = c_re.astype(F32), c_im.astype(F32)

    def c_lam(d, p_r, p_i):
        return (cre[d][None] * p_r[:, :, None, :] - cim[d][None] * p_i[:, :, None, :],
                cre[d][None] * p_i[:, :, None, :] + cim[d][None] * p_r[:, :, None, :])

    def lag_kernel(d, tau):
        p_r, p_i = lam_pow(tau)
        c_r, c_i = c_lam(d, p_r[d], p_i[d])
        return (jnp.einsum('tgop,gpi->tgoi', c_r, bbr[d]) - jnp.einsum('tgop,gpi->tgoi', c_i, bbi[d]))

    def octet(a, g_axis):
        return a.reshape(a.shape[:g_axis] + (S5_OCT, S5_GB) + a.shape[g_axis + 1:])

    kf = lag_kernel(0, jnp.arange(t))
    kb0 = lag_kernel(1, jnp.arange(1))
    kb = lag_kernel(1, t - 1 - jnp.arange(t - 1))
    kall = jnp.concatenate([kb, kf[:1] + kb0, kf[1:]], axis=0)
    kd = octet(kall, 1).transpose(1, 0, 3, 2, 4).reshape(S5_OCT, 2 * t - 1, S5_H, LANES)

    def wmat(d, tau):
        p_r, p_i = lam_pow(tau)
        p_r, p_i = p_r[d][:, :, :, None], p_i[d][:, :, :, None]
        wr = p_r * bbr[d][None] - p_i * bbi[d][None]
        wi = p_r * bbi[d][None] + p_i * bbr[d][None]
        to = lambda w: octet(w, 1).transpose(1, 0, 3, 2, 4).reshape(S5_OCT, t, S5_P, LANES)
        return [to(wr), to(wi)]

    wd = jnp.concatenate(wmat(0, t - 1 - jnp.arange(t)) + wmat(1, jnp.arange(t)), axis=1)

    def vmat(d, tau):
        p_r, p_i = lam_pow(tau)
        c_r, c_i = c_lam(d, p_r[d], p_i[d])
        to = lambda c: octet(c, 1).transpose(1, 0, 3, 2, 4).reshape(S5_OCT, t, S5_H, S5_SL)
        return [to(c_r), to(-c_i)]

    vd = jnp.concatenate(vmat(0, 1 + jnp.arange(t)) + vmat(1, t - jnp.arange(t)), axis=1)

    ltr, lti = lam_pow(jnp.full((1,), t))
    rows = [ltr[0, 0], lti[0, 0], ltr[1, 0], lti[1, 0]]
    rows = [r.reshape(S5_OCT, S5_SL) for r in rows] + [jnp.zeros((S5_OCT, S5_SL), F32)] * 4
    lam = jnp.stack(rows, axis=1)
    return kd, wd, vd, lam


def _s5_branch(u_bf, nb, seq, nctx_tok, mats):
    kd, wd, vd, lam = mats
    t = S5_T
    n_lat = nb * seq
    ul = u_bf[:n_lat].reshape(nb, seq // t, t, S5_OCT, LANES).transpose(3, 0, 1, 2, 4)
    uc = u_bf[n_lat:].reshape(nb, nctx_tok // t, t, S5_OCT, LANES).transpose(3, 0, 1, 2, 4)
    nctx = nctx_tok // t
    nc = nctx + seq // t
    u8 = jnp.concatenate([uc, ul], axis=2).reshape(S5_OCT, nb, nc, S5_TL)
    y8 = _s5_scan(u8, kd, wd, vd, lam, nb, nc, nctx).reshape(S5_OCT, nb, nc, t, LANES)
    yc = y8[:, :, :nctx].transpose(1, 2, 3, 0, 4).reshape(nb * nctx_tok, D_S5)
    yl = y8[:, :, nctx:].transpose(1, 2, 3, 0, 4).reshape(n_lat, D_S5)
    return jnp.concatenate([yl, yc], axis=0)


def _hgrn_dir(q_ref, v_ref, f_ref, lbv, o_ref, s_scr, b, d, backward):
    c = HG_C
    x = f_ref[...]
    f = lbv + (1.0 - lbv) * jax.nn.sigmoid(x)
    logf = jnp.log(f)
    kk = (1.0 - lbv) * jax.nn.sigmoid(-x)
    row = lax.broadcasted_iota(jnp.int32, (c, c), 0)
    col = lax.broadcasted_iota(jnp.int32, (c, c), 1)
    keep = (col >= row) if backward else (col <= row)
    tri = keep.astype(BF16)
    cum = jnp.zeros((c, D_HG), F32)
    for piece in _split3(logf):
        cum = cum + jnp.dot(tri, piece, preferred_element_type=F32)
    tot = cum[0:1] if backward else cum[c - 1:c]
    q = q_ref[...]
    v = v_ref[...]
    qe = q * jnp.exp(cum)
    kh = kk * jnp.exp(tot - cum)
    etot = jnp.exp(tot)
    nsub = c // HG_SUB
    for h in range(HG_HEADS):
        sl = slice(h * HG_DK, (h + 1) * HG_DK)
        qh, kh_h, vh, cumh = q[:, sl], kk[:, sl], v[:, sl], cum[:, sl]
        vb16 = vh.astype(BF16)
        st = s_scr[b, d, h]
        inter = lax.dot_general(qe[:, sl].astype(BF16), st.astype(BF16),
                                (((1,), (1,)), ((), ())), preferred_element_type=F32)
        blocks = []
        for i in range(nsub):
            lo, hi = i * HG_SUB, (i + 1) * HG_SUB
            if backward:
                ref = cumh[hi:hi + 1] if i < nsub - 1 else jnp.zeros((1, HG_DK), F32)
            else:
                ref = cumh[lo - 1:lo] if i > 0 else jnp.zeros((1, HG_DK), F32)
            qt = qh[lo:hi] * jnp.exp(cumh[lo:hi] - ref)
            kt = kh_h * jnp.exp(jnp.minimum(ref - cumh, HG_CLAMP))
            blocks.append(lax.dot_general(qt.astype(BF16), kt.astype(BF16),
                                          (((1,), (1,)), ((), ())), preferred_element_type=F32))
        scores = jnp.where(keep, jnp.concatenate(blocks, axis=0), 0.0)
        o_ref[b, :, sl] = inter + jnp.dot(scores.astype(BF16), vb16, preferred_element_type=F32)
        upd = lax.dot_general(vb16, kh[:, sl].astype(BF16),
                              (((0,), (0,)), ((), ())), preferred_element_type=F32)
        s_scr[b, d, h] = st * etot[:, sl] + upd


def _hgrn_kernel(*refs, nb):
    zin = refs[:6 * nb]
    lb_ref, s0_ref, of_ref, ob_ref, sfin_ref, s_scr = refs[6 * nb:]
    i = pl.program_id(0)

    @pl.when(i == 0)
    def _():
        s_scr[...] = s0_ref[...]

    for b in range(nb):
        qf, vf, ff, qb, vb, fb = zin[6 * b:6 * b + 6]
        _hgrn_dir(qf, vf, ff, lb_ref[0:1], of_ref, s_scr, b, 0, False)
        _hgrn_dir(qb, vb, fb, lb_ref[1:2], ob_ref, s_scr, b, 1, True)

    @pl.when(i == pl.num_programs(0) - 1)
    def _():
        sfin_ref[...] = s_scr[...]


def _hgrn_scan(zview, lb2, s0, nb, nsteps, in_f, in_b, out_f, out_b, out_view_shape):
    blk = (HG_C, D_HG)
    oblk = (nb, HG_C, D_HG)
    state_shape = (nb, 2, HG_HEADS, HG_DK, HG_DK)
    state_spec = pl.BlockSpec(state_shape, lambda i: (0, 0, 0, 0, 0))
    in_specs = []
    for b in range(nb):
        in_specs += [pl.BlockSpec(blk, lambda i, b=b, cb=cb: in_f(b, i, cb)) for cb in (1, 2, 3)]
        in_specs += [pl.BlockSpec(blk, lambda i, b=b, cb=cb: in_b(b, i, cb)) for cb in (1, 2, 4)]
    in_specs += [pl.BlockSpec((2, D_HG), lambda i: (0, 0)), state_spec]
    return pl.pallas_call(
        functools.partial(_hgrn_kernel, nb=nb),
        grid=(nsteps,),
        in_specs=in_specs,
        out_specs=[pl.BlockSpec(oblk, lambda i: (0,) + out_f(i)),
                   pl.BlockSpec(oblk, lambda i: (0,) + out_b(i)), state_spec],
        out_shape=[jax.ShapeDtypeStruct(out_view_shape, F32),
                   jax.ShapeDtypeStruct(out_view_shape, F32),
                   jax.ShapeDtypeStruct(state_shape, F32)],
        scratch_shapes=[pltpu.VMEM(state_shape, F32)],
        compiler_params=_cparams(("arbitrary",)),
        name="hgrn_scan",
    )(*([zview] * (6 * nb) + [lb2, s0]))


def _hgrn_branch(z, nb, seq, nctx_tok, lb2):
    n_lat = nb * seq
    c = HG_C
    nsc = nctx_tok // c
    base_c = n_lat // c
    s0 = jnp.zeros((nb, 2, HG_HEADS, HG_DK, HG_DK), F32)
    oc_f, oc_b, s_ctx = _hgrn_scan(
        z, lb2, s0, nb, nsc,
        lambda b, i, cb: (base_c + b * nsc + i, cb),
        lambda b, i, cb: (base_c + b * nsc + (nsc - 1 - i), cb),
        lambda i: (i, 0),
        lambda i: (nsc - 1 - i, 0),
        (nb, nctx_tok, D_HG))
    nsl = seq // c
    ol_f, ol_b, _ = _hgrn_scan(
        z, lb2, s_ctx, nb, nsl,
        lambda b, i, cb: (b * nsl + i, cb),
        lambda b, i, cb: (b * nsl + (nsl - 1 - i), cb),
        lambda i: (i, 0),
        lambda i: (nsl - 1 - i, 0),
        (nb, seq, D_HG))
    return (ol_f.reshape(n_lat, D_HG), ol_b.reshape(n_lat, D_HG),
            oc_f.reshape(nb * nctx_tok, D_HG), oc_b.reshape(nb * nctx_tok, D_HG))


def _gelu_tanh(x):
    return 0.5 * x * (1.0 + jnp.tanh(math.sqrt(2.0 / math.pi) * (x + 0.044715 * (x * x * x))))


def _post_kernel(x_ref, u_ref, g_ref, gs_ref, gh_ref, y5_ref, ofl_ref, obl_ref, ofc_ref, obc_ref,
                 m_ref, d_ref, wglu_ref, hn_ref, wus_ref, wuh_ref, wo_ref, n2_ref,
                 rwh_ref, rwl_ref, rb_ref,
                 xo_ref, h2_ref, gate_ref, meta_ref, cnt_ref, run_scr, *, n_lat_tiles):
    i = pl.program_id(0)

    @pl.when(i == 0)
    def _():
        run_scr[...] = jnp.zeros_like(run_scr)

    m = m_ref[0]
    zz = _gelu_tanh(y5_ref[...] + d_ref[...] * u_ref[...])
    glu = jnp.dot(zz.astype(BF16), wglu_ref[...], preferred_element_type=F32)
    y5 = zz * jax.nn.sigmoid(glu)
    o = jnp.where(i < n_lat_tiles, ofl_ref[...] + obl_ref[...], ofc_ref[...] + obc_ref[...])
    hn = hn_ref[...]
    on = jnp.concatenate(
        [_rms(o[:, h * HG_DK:(h + 1) * HG_DK], hn) for h in range(HG_HEADS)], axis=1)
    g = g_ref[...]
    yh = on * (g * jax.nn.sigmoid(g))
    merged = (jax.nn.sigmoid(gs_ref[...])
              * jnp.dot(y5.astype(BF16), wus_ref[...], preferred_element_type=F32)
              + jax.nn.sigmoid(gh_ref[...])
              * jnp.dot(yh.astype(BF16), wuh_ref[...], preferred_element_type=F32))
    y = jnp.dot(merged.astype(BF16), wo_ref[...], preferred_element_type=F32)
    xn = x_ref[...] + m[:, 2 * D:3 * D] * y
    xo_ref[...] = xn
    h2 = _rms(xn, n2_ref[...]) * (1.0 + m[:, 4 * D:5 * D]) + m[:, 3 * D:4 * D]
    h2_ref[...] = h2

    hi = h2.astype(BF16)
    lo = (h2 - hi.astype(F32)).astype(BF16)
    rwh = rwh_ref[...]
    logits = (jnp.dot(hi, rwh, preferred_element_type=F32)
              + jnp.dot(lo, rwh, preferred_element_type=F32)
              + jnp.dot(hi, rwl_ref[...], preferred_element_type=F32)) + rb_ref[...]
    tm = logits.shape[0]
    lane = lax.broadcasted_iota(jnp.int32, (tm, LANES), 1)
    lane_f = lane.astype(F32)
    l = logits
    vals, idxs, hots = [], [], []
    for _ in range(TOP_K):
        mk = jnp.max(l, axis=1, keepdims=True)
        ik = jnp.min(jnp.where(l == mk, lane_f, float(LANES)), axis=1, keepdims=True)
        hot = lane_f == ik
        vals.append(mk)
        idxs.append(ik)
        hots.append(hot)
        l = jnp.where(hot, NEG * 10.0, l)
    exps = [jnp.exp(vk - vals[0]) for vk in vals]
    den = exps[0] + exps[1] + exps[2] + exps[3]
    sel = (hots[0] | hots[1] | hots[2] | hots[3]).astype(F32)
    r_i = lax.broadcasted_iota(jnp.int32, (tm, tm), 0)
    c_i = lax.broadcasted_iota(jnp.int32, (tm, tm), 1)
    strict = (c_i < r_i).astype(BF16)
    pos = jnp.dot(strict, sel.astype(BF16), preferred_element_type=F32) + run_scr[...]
    run_scr[...] += jnp.sum(sel, axis=0, keepdims=True)
    cnt_ref[...] = run_scr[...]
    gate = jnp.zeros((tm, LANES), F32)
    meta = jnp.zeros((tm, LANES), jnp.int32)
    for k in range(TOP_K):
        pk = jnp.sum(jnp.where(hots[k], pos, 0.0), axis=1, keepdims=True).astype(jnp.int32)
        gate = jnp.where(lane == k, exps[k] / den, gate)
        meta = jnp.where(lane == k, idxs[k].astype(jnp.int32), meta)
        meta = jnp.where(lane == TOP_K + k, pk, meta)
    gate_ref[...] = gate
    meta_ref[...] = meta


def _post(xall, z, y5s, hg_out, mods3, mod_row, p, n_lat_tiles):
    n = xall.shape[0]
    tile = lambda w, cb: pl.BlockSpec((TM, w), lambda i: (i, cb))
    full = lambda a: pl.BlockSpec(a.shape, lambda i: (0,) * a.ndim)
    lat_tile = pl.BlockSpec((TM, D_HG), lambda i: (jnp.minimum(i, n_lat_tiles - 1), 0))
    ctx_tile = pl.BlockSpec((TM, D_HG), lambda i: (jnp.maximum(i - n_lat_tiles, 0), 0))
    consts = [p['d'], p['wglu'], p['hn'], p['wus'], p['wuh'], p['wo'], p['n2'],
              p['rwh'], p['rwl'], p['rb']]
    return pl.pallas_call(
        functools.partial(_post_kernel, n_lat_tiles=n_lat_tiles),
        grid=(n // TM,),
        in_specs=[
            tile(D, 0),
            tile(D_S5, 0),
            tile(D_HG, 5),
            tile(D, 3),
            tile(D, 4),
            tile(D_S5, 0), lat_tile, lat_tile, ctx_tile, ctx_tile,
            pl.BlockSpec((1, 1, N_MOD * D), lambda i: (mod_row(i), 0, 0)),
        ] + [full(a) for a in consts],
        out_specs=[tile(D, 0), tile(D, 0), tile(LANES, 0), tile(LANES, 0),
                   pl.BlockSpec((1, LANES), lambda i: (0, 0))],
        out_shape=[jax.ShapeDtypeStruct((n, D), F32), jax.ShapeDtypeStruct((n, D), F32),
                   jax.ShapeDtypeStruct((n, LANES), F32), jax.ShapeDtypeStruct((n, LANES), jnp.int32),
                   jax.ShapeDtypeStruct((1, LANES), F32)],
        scratch_shapes=[pltpu.VMEM((1, LANES), F32)],
        compiler_params=_cparams(("arbitrary",)),
        name="mixer_post",
    )(xall, z, z, z, z, y5s, *hg_out, mods3, *consts)


def _dispatch_kernel(dest_ref, h_ref, xs_in, xs_ref, sem):
    del xs_in

    def body(r, carry):
        for k in range(TOP_K):
            dst = dest_ref[0, 0, r * TOP_K + k]
            pltpu.make_async_copy(h_ref.at[pl.ds(r, 1)], xs_ref.at[pl.ds(dst, 1)],
                                  sem).start(priority=k % 2)
        return carry

    lax.fori_loop(0, TM, body, 0, unroll=DMA_UNROLL)
    for _ in range(TOP_K):
        pltpu.make_async_copy(h_ref, h_ref, sem).wait()


def _dispatch(h2, dest3, xs0):
    n = h2.shape[0]
    return pl.pallas_call(
        _dispatch_kernel,
        grid=(n // TM,),
        in_specs=[
            pl.BlockSpec((1, 1, TM * TOP_K), lambda i: (i, 0, 0), memory_space=pltpu.SMEM),
            pl.BlockSpec((TM, D), lambda i: (i, 0)),
            pl.BlockSpec(memory_space=pl.ANY),
        ],
        out_specs=pl.BlockSpec(memory_space=pl.ANY),
        out_shape=jax.ShapeDtypeStruct(xs0.shape, F32),
        scratch_shapes=[pltpu.SemaphoreType.DMA(())],
        input_output_aliases={2: 0},
        compiler_params=pltpu.CompilerParams(dimension_semantics=("arbitrary",),
                                             vmem_limit_bytes=VMEM_LIMIT, has_side_effects=True),
        name="moe_dispatch",
    )(dest3, h2, xs0)


def _expert_kernel(be_ref, nu_ref, xs_ref, w1_ref, b1_ref, w2_ref, b2_ref, ys_ref, w1_scr, w2_scr):
    i = pl.program_id(0)
    prev = be_ref[jnp.maximum(i - 1, 0)]

    @pl.when((i == 0) | (be_ref[i] != prev))
    def _():
        rows = 256
        for r in range(0, D, rows):
            w1_scr[r:r + rows, :] = w1_ref[0, r:r + rows, :].astype(BF16)
        for r in range(0, D_FF, rows):
            w2_scr[r:r + rows, :] = w2_ref[0, r:r + rows, :].astype(BF16)

    @pl.when(i < nu_ref[0])
    def _():
        gu = jnp.dot(xs_ref[...].astype(BF16), w1_scr[...], preferred_element_type=F32) + b1_ref[0]
        g = jnp.minimum(gu[:, :D_FF], SWIGLU_LIMIT)
        u = jnp.clip(gu[:, D_FF:], -SWIGLU_LIMIT, SWIGLU_LIMIT)
        a = g * jax.nn.sigmoid(SWIGLU_ALPHA * g) * (u + 1.0)
        ys_ref[...] = jnp.dot(a.astype(BF16), w2_scr[...], preferred_element_type=F32) + b2_ref[0]

    @pl.when(i >= nu_ref[0])
    def _():
        ys_ref[...] = jnp.zeros_like(ys_ref)


def _experts(block_e, n_used, xs, layer, w1, b1, w2, b2):
    n_rows = xs.shape[0]
    grid_spec = pltpu.PrefetchScalarGridSpec(
        num_scalar_prefetch=2,
        grid=(n_rows // MOE_BLK,),
        in_specs=[
            pl.BlockSpec((MOE_BLK, D), lambda i, be, nu: (i, 0)),
            pl.BlockSpec((None, 1, D, 2 * D_FF), lambda i, be, nu: (layer, be[i], 0, 0)),
            pl.BlockSpec((None, 1, 1, 2 * D_FF), lambda i, be, nu: (layer, be[i], 0, 0)),
            pl.BlockSpec((None, 1, D_FF, D), lambda i, be, nu: (layer, be[i], 0, 0)),
            pl.BlockSpec((None, 1, 1, D), lambda i, be, nu: (layer, be[i], 0, 0)),
        ],
        out_specs=pl.BlockSpec((MOE_BLK, D), lambda i, be, nu: (i, 0)),
        scratch_shapes=[pltpu.VMEM((D, 2 * D_FF), BF16), pltpu.VMEM((D_FF, D), BF16)],
    )
    return pl.pallas_call(
        _expert_kernel,
        grid_spec=grid_spec,
        out_shape=jax.ShapeDtypeStruct((n_rows, D), F32),
        compiler_params=_cparams(("arbitrary",)),
        name="moe_experts",
    )(block_e, n_used, xs, w1, b1, w2, b2)


def _combine_kernel(dest_ref, x_ref, gate_ref, m_ref, ys_ref, xo_ref, buf, sem):
    def body(r, carry):
        for k in range(TOP_K):
            src = dest_ref[0, 0, r * TOP_K + k]
            pltpu.make_async_copy(ys_ref.at[pl.ds(src, 1)], buf.at[k, pl.ds(r, 1)],
                                  sem).start(priority=k % 2)
        return carry

    lax.fori_loop(0, TM, body, 0, unroll=DMA_UNROLL)
    for k in range(TOP_K):
        pltpu.make_async_copy(buf.at[k], buf.at[k], sem).wait()
    gate = gate_ref[...]
    y = gate[:, 0:1] * buf[0]
    for k in range(1, TOP_K):
        y = y + gate[:, k:k + 1] * buf[k]
    xo_ref[...] = x_ref[...] + m_ref[0][:, 5 * D:6 * D] * y


def _combine(xn, gate, dest3, ys, mods3, mod_row):
    n = xn.shape[0]
    return pl.pallas_call(
        _combine_kernel,
        grid=(n // TM,),
        in_specs=[
            pl.BlockSpec((1, 1, TM * TOP_K), lambda i: (i, 0, 0), memory_space=pltpu.SMEM),
            pl.BlockSpec((TM, D), lambda i: (i, 0)),
            pl.BlockSpec((TM, LANES), lambda i: (i, 0)),
            pl.BlockSpec((1, 1, N_MOD * D), lambda i: (mod_row(i), 0, 0)),
            pl.BlockSpec(memory_space=pl.ANY),
        ],
        out_specs=pl.BlockSpec((TM, D), lambda i: (i, 0)),
        out_shape=jax.ShapeDtypeStruct((n, D), F32),
        scratch_shapes=[pltpu.VMEM((TOP_K, TM, D), F32), pltpu.SemaphoreType.DMA(())],
        compiler_params=_cparams(("arbitrary",)),
        name="moe_combine",
    )(dest3, xn, gate, mods3, ys)


def _final_norm_kernel(x_ref, g_ref, o_ref):
    o_ref[...] = _rms(x_ref[...], g_ref[...])


def _final_norm(xall, g, n_lat):
    return pl.pallas_call(
        _final_norm_kernel,
        grid=(n_lat // TM,),
        in_specs=[pl.BlockSpec((TM, D), lambda i: (i, 0)), pl.BlockSpec((1, D), lambda i: (0, 0))],
        out_specs=pl.BlockSpec((TM, D), lambda i: (i, 0)),
        out_shape=jax.ShapeDtypeStruct((n_lat, D), F32),
        compiler_params=_cparams(("arbitrary",)),
        name="final_norm",
    )(xall, g.reshape(1, D))


def kernel(x, c, ctx, c_ctx, mod_w, mod_b, norm1_g, norm2_g, w_in, s5_a_re, s5_a_im, s5_log_dt,
           s5_b_re, s5_b_im, s5_c_re, s5_c_im, s5_d, s5_w_glu, hgrn_lb_logits, hgrn_norm_g,
           w_up_s5, w_up_hgrn, w_out, router_w, router_b, moe_w1, moe_b1, moe_w2, moe_b2, final_g):
    nb, seq, d = x.shape
    nctx_tok = ctx.shape[1]
    depth = mod_w.shape[0]
    assert d == D and seq % (GRID_W * HG_C) == 0 and nctx_tok % TM == 0 and nb + 1 <= 8
    assert nctx_tok % (8 * S5_T) == 0 and seq % (8 * S5_T) == 0
    n_lat = nb * seq
    n = n_lat + nb * nctx_tok
    tiles_per_batch = seq // TM
    n_lat_tiles = n_lat // TM

    def mod_row(i):
        return jnp.where(i < n_lat_tiles, i // tiles_per_batch, nb)

    cpad = jnp.zeros((8, D), F32).at[:nb].set(c).at[nb].set(c_ctx)
    mods = _modulation(cpad, mod_w, mod_b)
    p_lb = jax.nn.softmax(hgrn_lb_logits.astype(F32), axis=1)
    lb_all = jnp.cumsum(p_lb, axis=1)[:, :depth]

    n_rows = pl.cdiv(n * TOP_K, MOE_BLK) * MOE_BLK + N_EXP * MOE_BLK
    n_blocks = n_rows // MOE_BLK

    rows = seq // GRID_W
    b1_all = moe_b1.reshape(depth, N_EXP, 1, 2 * D_FF)
    b2_all = moe_b2.reshape(depth, N_EXP, 1, D)

    def regrid(xa, to_colmajor):
        shape = (nb, rows, GRID_W, D) if to_colmajor else (nb, GRID_W, rows, D)
        lat = xa[:n_lat].reshape(shape).transpose(0, 2, 1, 3).reshape(n_lat, D)
        return xa.at[:n_lat].set(lat)

    xall = jnp.concatenate([x.reshape(n_lat, D), ctx.reshape(nb * nctx_tok, D)], axis=0)
    colmajor = False
    for l in range(depth):
        if (l % 2 == 1) != colmajor:
            xall = regrid(xall, not colmajor)
            colmajor = not colmajor
        mods3 = mods[l, :nb + 1].reshape(nb + 1, 1, N_MOD * D)
        z, u_bf = _inproj(xall, mods3, norm1_g[l], w_in[l].astype(BF16), mod_row)
        mats = _s5_matrices(s5_a_re[l], s5_a_im[l], s5_log_dt[l], s5_b_re[l], s5_b_im[l],
                            s5_c_re[l], s5_c_im[l])
        y5s = _s5_branch(u_bf, nb, seq, nctx_tok, mats)
        hg_out = _hgrn_branch(z, nb, seq, nctx_tok, lb_all[:, l])
        rw = jnp.zeros((D, LANES), F32).at[:, :N_EXP].set(router_w[l])
        rwh = rw.astype(BF16)
        rwl = (rw - rwh.astype(F32)).astype(BF16)
        rb = jnp.full((1, LANES), NEG, F32).at[0, :N_EXP].set(router_b[l])
        params = dict(d=s5_d[l].reshape(1, D_S5), wglu=s5_w_glu[l].astype(BF16),
                      hn=hgrn_norm_g[l].reshape(1, HG_DK), wus=w_up_s5[l].astype(BF16),
                      wuh=w_up_hgrn[l].astype(BF16), wo=w_out[l].astype(BF16),
                      n2=norm2_g[l].reshape(1, D), rwh=rwh, rwl=rwl, rb=rb)
        xn, h2, gate, meta, cnt = _post(xall, z, y5s, hg_out, mods3, mod_row, params, n_lat_tiles)
        counts = cnt[0, :N_EXP].astype(jnp.int32)
        padded = (counts + MOE_BLK - 1) // MOE_BLK * MOE_BLK
        pad_end = jnp.cumsum(padded)
        pad_start = pad_end - padded
        dest = pad_start[meta[:, :TOP_K]] + meta[:, TOP_K:2 * TOP_K]
        dest3 = dest.reshape(n // TM, 1, TM * TOP_K)
        blk_start = jnp.arange(n_blocks, dtype=jnp.int32) * MOE_BLK
        block_e = jnp.minimum(
            jnp.sum((pad_end[None, :] <= blk_start[:, None]).astype(jnp.int32), axis=1), N_EXP - 1)
        n_used = (pad_end[-1:] // MOE_BLK).astype(jnp.int32)
        xs = _dispatch(h2, dest3, jnp.zeros((n_rows, D), F32))
        ys = _experts(block_e, n_used, xs, l, moe_w1, b1_all, moe_w2, b2_all)
        xall = _combine(xn, gate, dest3, ys, mods3, mod_row)
    if colmajor:
        xall = regrid(xall, False)
    return _final_norm(xall, final_g, n_lat).reshape(nb, seq, D)
```

```python
import functools
import math

import jax
import jax.numpy as jnp
from jax import lax
from jax.experimental import pallas as pl
from jax.experimental.pallas import tpu as pltpu

F32 = jnp.float32
BF16 = jnp.bfloat16

D = 1024
GRID_W = 64
D_S5 = 512
S5_H = 16
S5_G = D_S5 // S5_H
S5_P = 64
D_HG = 512
HG_HEADS = 4
HG_DK = 128
N_IN = D_S5 + 5 * D_HG + 2 * D
N_EXP = 32
TOP_K = 4
D_FF = D
SWIGLU_LIMIT = 7.0
SWIGLU_ALPHA = 1.702
N_MOD = 6
EPS = 1e-6

LANES = 128
TM = 256
S5_T = 16
S5_GB = LANES // S5_H
S5_OCT = S5_G // S5_GB
S5_TL = S5_T * LANES
S5_SL = S5_GB * S5_P
HG_C = 64
HG_SUB = 16
HG_CLAMP = 80.0
MOE_BLK = 256
DMA_UNROLL = 4
VMEM_LIMIT = 56 * 1024 * 1024
NEG = -1e30


def _cparams(sem, vmem=VMEM_LIMIT):
    return pltpu.CompilerParams(dimension_semantics=sem, vmem_limit_bytes=vmem)


def _rms(x, g):
    return x * lax.rsqrt(jnp.mean(x * x, axis=-1, keepdims=True) + EPS) * g


def _split3(x):
    hi = x.astype(BF16)
    r1 = x - hi.astype(F32)
    mid = r1.astype(BF16)
    lo = (r1 - mid.astype(F32)).astype(BF16)
    return hi, mid, lo


def _mod_kernel(c_ref, w_ref, b_ref, o_ref):
    c = c_ref[...]
    sc = c * jax.nn.sigmoid(c)
    acc = jnp.zeros(o_ref.shape[1:], F32)
    w = w_ref[0]
    for piece in _split3(sc):
        for wp in _split3(w):
            acc = acc + jnp.dot(piece, wp, preferred_element_type=F32)
    o_ref[0] = acc + b_ref[0]


def _modulation(cpad, mod_w, mod_b):
    depth = mod_w.shape[0]
    tn = 1536
    return pl.pallas_call(
        _mod_kernel,
        grid=(depth, (N_MOD * D) // tn),
        in_specs=[
            pl.BlockSpec((8, D), lambda l, j: (0, 0)),
            pl.BlockSpec((1, D, tn), lambda l, j: (l, 0, j)),
            pl.BlockSpec((1, 1, tn), lambda l, j: (l, 0, j)),
        ],
        out_specs=pl.BlockSpec((1, 8, tn), lambda l, j: (l, 0, j)),
        out_shape=jax.ShapeDtypeStruct((depth, 8, N_MOD * D), F32),
        compiler_params=_cparams(("arbitrary", "arbitrary")),
        name="modulation",
    )(cpad, mod_w, mod_b.reshape(depth, 1, N_MOD * D))


def _inproj_kernel(x_ref, m_ref, g_ref, w_ref, z_ref, u_ref):
    m = m_ref[0]
    h = _rms(x_ref[...], g_ref[...])
    h = h * (1.0 + m[:, D:2 * D]) + m[:, 0:D]
    z = jnp.dot(h.astype(BF16), w_ref[...], preferred_element_type=F32)
    z_ref[...] = z
    u_ref[...] = z[:, :D_S5].astype(BF16)


def _inproj(xall, mods3, g, w_bf, mod_row):
    n = xall.shape[0]
    return pl.pallas_call(
        _inproj_kernel,
        grid=(n // TM,),
        in_specs=[
            pl.BlockSpec((TM, D), lambda i: (i, 0)),
            pl.BlockSpec((1, 1, N_MOD * D), lambda i: (mod_row(i), 0, 0)),
            pl.BlockSpec((1, D), lambda i: (0, 0)),
            pl.BlockSpec((D, N_IN), lambda i: (0, 0)),
        ],
        out_specs=[pl.BlockSpec((TM, N_IN), lambda i: (i, 0)),
                   pl.BlockSpec((TM, D_S5), lambda i: (i, 0))],
        out_shape=[jax.ShapeDtypeStruct((n, N_IN), F32), jax.ShapeDtypeStruct((n, D_S5), BF16)],
        compiler_params=_cparams(("arbitrary",)),
        name="inproj",
    )(xall, mods3, g.reshape(1, D), w_bf)


def _s5_row_block(nc):
    return max(r for r in range(16, 257, 16) if nc % r == 0)


def _s5_expand(kd_ref, wd_ref, vd_ref, mcat_scr, v_scr):
    t = S5_T

    def rep(n_in, n_out):
        r = lax.broadcasted_iota(jnp.int32, (n_in, n_out), 0)
        c = lax.broadcasted_iota(jnp.int32, (n_in, n_out), 1)
        return (c % n_in == r).astype(BF16)

    def group_mask(rows, cols, rdiv, cdiv):
        r = lax.broadcasted_iota(jnp.int32, (rows, cols), 0)
        c = lax.broadcasted_iota(jnp.int32, (rows, cols), 1)
        return r // rdiv == c // cdiv

    def widen(compact, r01, mask):
        full = lax.dot_general(compact.astype(BF16), r01, (((0,), (0,)), ((), ())),
                               preferred_element_type=F32)
        return jnp.where(mask, full, 0.0).astype(BF16)

    r_h, r_p = rep(S5_H, LANES), rep(S5_P, S5_SL)
    m_mask = group_mask(LANES, LANES, S5_H, S5_H)
    w_mask = group_mask(LANES, S5_SL, S5_H, S5_P)
    v_mask = group_mask(S5_SL, LANES, S5_P, S5_H)
    for lag in range(2 * t - 1):
        tile = widen(kd_ref[0, lag], r_h, m_mask)
        for s in range(t):
            tt = s + lag - (t - 1)
            if 0 <= tt < t:
                mcat_scr[s * LANES:(s + 1) * LANES, tt * LANES:(tt + 1) * LANES] = tile
    for kind in range(4):
        for s in range(t):
            c0 = S5_TL + kind * S5_SL
            mcat_scr[s * LANES:(s + 1) * LANES, c0:c0 + S5_SL] = widen(
                wd_ref[0, kind * t + s], r_p, w_mask)
            v_scr[kind * S5_SL:(kind + 1) * S5_SL, s * LANES:(s + 1) * LANES] = widen(
                vd_ref[0, kind * t + s], r_h, v_mask)


def _s5_kernel(u_ref, kd_ref, wd_ref, vd_ref, lam_ref, y_ref, d_scr, mcat_scr, v_scr, *, nc, nctx):
    @pl.when(pl.program_id(1) == 0)
    def _():
        _s5_expand(kd_ref, wd_ref, vd_ref, mcat_scr, v_scr)

    rb = _s5_row_block(nc)
    for r0 in range(0, nc, rb):
        ub = u_ref[0, 0, r0:r0 + rb, :]
        y_ref[0, 0, r0:r0 + rb, :] = jnp.dot(ub, mcat_scr[:, :S5_TL], preferred_element_type=F32)
        d_scr[r0:r0 + rb, :] = jnp.dot(ub, mcat_scr[:, S5_TL:], preferred_element_type=F32)
    lam = lam_ref[0]
    sub = 8
    zero = jnp.zeros((sub, S5_SL), F32)
    srow = lax.broadcasted_iota(jnp.int32, (sub, S5_SL), 0)
    ntile, nctile = nc // sub, nctx // sub

    def bcast(row):
        return jnp.broadcast_to(row, (sub, S5_SL))

    def run(col, lr, li, tile_of, descending):
        def body(j, carry):
            xr, xi = carry
            base = pl.multiple_of(tile_of(j) * sub, sub)
            dt = d_scr[pl.ds(base, sub), col:col + 2 * S5_SL]
            accr, acci = zero, zero
            for step in range(sub):
                s = sub - 1 - step if descending else step
                accr = jnp.where(srow == s, xr, accr)
                acci = jnp.where(srow == s, xi, acci)
                dr = bcast(dt[s:s + 1, 0:S5_SL])
                di = bcast(dt[s:s + 1, S5_SL:2 * S5_SL])
                xr, xi = lr * xr - li * xi + dr, lr * xi + li * xr + di
            d_scr[pl.ds(base, sub), col:col + S5_SL] = accr
            d_scr[pl.ds(base, sub), col + S5_SL:col + 2 * S5_SL] = acci
            return xr, xi

        lax.fori_loop(0, ntile, body, (zero, zero))

    run(0, lam[0:1], lam[1:2], lambda j: j, False)
    run(2 * S5_SL, lam[2:3], lam[3:4],
        lambda j: jnp.where(j < nctile, nctile - 1 - j, ntile - 1 - (j - nctile)), True)
    for r0 in range(0, nc, rb):
        y_ref[0, 0, r0:r0 + rb, :] += jnp.dot(d_scr[r0:r0 + rb, :].astype(BF16), v_scr[...],
                                               preferred_element_type=F32)


def _s5_scan(u8, kd, wd, vd, lam, nb, nc, nctx):
    kern = functools.partial(_s5_kernel, nc=nc, nctx=nctx)
    once = pl.Buffered(1)
    t = S5_T
    return pl.pallas_call(
        kern,
        grid=(S5_OCT, nb),
        in_specs=[
            pl.BlockSpec((1, 1, nc, S5_TL), lambda o, b: (o, b, 0, 0)),
            pl.BlockSpec((1, 2 * t - 1, S5_H, LANES), lambda o, b: (o, 0, 0, 0), pipeline_mode=once),
            pl.BlockSpec((1, 4 * t, S5_P, LANES), lambda o, b: (o, 0, 0, 0), pipeline_mode=once),
            pl.BlockSpec((1, 4 * t, S5_H, S5_SL), lambda o, b: (o, 0, 0, 0), pipeline_mode=once),
            pl.BlockSpec((1, 8, S5_SL), lambda o, b: (o, 0, 0)),
        ],
        out_specs=pl.BlockSpec((1, 1, nc, S5_TL), lambda o, b: (o, b, 0, 0), pipeline_mode=once),
        out_shape=jax.ShapeDtypeStruct((S5_OCT, nb, nc, S5_TL), F32),
        scratch_shapes=[pltpu.VMEM((nc, 4 * S5_SL), F32),
                        pltpu.VMEM((S5_TL, S5_TL + 4 * S5_SL), BF16),
                        pltpu.VMEM((4 * S5_SL, S5_TL), BF16)],
        compiler_params=_cparams(("arbitrary", "arbitrary")),
        name="s5_scan",
    )(u8, kd, wd, vd, lam)


def _s5_matrices(a_re, a_im, log_dt, b_re, b_im, c_re, c_im):
    t = S5_T
    dt = jnp.exp(log_dt.astype(F32))[..., None]
    are, aim = a_re.astype(F32), a_im.astype(F32)

    def lam_pow(tau):
        tau = tau.astype(F32)[None, :, None, None]
        mag = jnp.exp(are[:, None] * dt[:, None] * tau)
        ang = aim[:, None] * dt[:, None] * tau
        return mag * jnp.cos(ang), mag * jnp.sin(ang)

    mag1 = jnp.exp(are * dt)
    lbr, lbi = mag1 * jnp.cos(aim * dt), mag1 * jnp.sin(aim * dt)
    den = are * are + aim * aim
    nr = lbr - 1.0
    qr = ((nr * are + lbi * aim) / den)[..., None]
    qi = ((lbi * are - nr * aim) / den)[..., None]
    bre, bim = b_re.astype(F32), b_im.astype(F32)
    bbr = qr * bre - qi * bim
    bbi = qr * bim + qi * bre
    cre, cim = c_re.astype(F32), c_im.astype(F32)

    def c_lam(d, p_r, p_i):
        return (cre[d][None] * p_r[:, :, None, :] - cim[d][None] * p_i[:, :, None, :],
                cre[d][None] * p_i[:, :, None, :] + cim[d][None] * p_r[:, :, None, :])

    def lag_kernel(d, tau):
        p_r, p_i = lam_pow(tau)
        c_r, c_i = c_lam(d, p_r[d], p_i[d])
        return (jnp.einsum('tgop,gpi->tgoi', c_r, bbr[d]) - jnp.einsum('tgop,gpi->tgoi', c_i, bbi[d]))

    def octet(a, g_axis):
        return a.reshape(a.shape[:g_axis] + (S5_OCT, S5_GB) + a.shape[g_axis + 1:])

    kf = lag_kernel(0, jnp.arange(t))
    kb0 = lag_kernel(1, jnp.arange(1))
    kb = lag_kernel(1, t - 1 - jnp.arange(t - 1))
    kall = jnp.concatenate([kb, kf[:1] + kb0, kf[1:]], axis=0)
    kd = octet(kall, 1).transpose(1, 0, 3, 2, 4).reshape(S5_OCT, 2 * t - 1, S5_H, LANES)

    def wmat(d, tau):
        p_r, p_i = lam_pow(tau)
        p_r, p_i = p_r[d][:, :, :, None], p_i[d][:, :, :, None]
        wr = p_r * bbr[d][None] - p_i * bbi[d][None]
        wi = p_r * bbi[d][None] + p_i * bbr[d][None]
        to = lambda w: octet(w, 1).transpose(1, 0, 3, 2, 4).reshape(S5_OCT, t, S5_P, LANES)
        return [to(wr), to(wi)]

    wd = jnp.concatenate(wmat(0, t - 1 - jnp.arange(t)) + wmat(1, jnp.arange(t)), axis=1)

    def vmat(d, tau):
        p_r, p_i = lam_pow(tau)
        c_r, c_i = c_lam(d, p_r[d], p_i[d])
        to = lambda c: octet(c, 1).transpose(1, 0, 3, 2, 4).reshape(S5_OCT, t, S5_H, S5_SL)
        return [to(c_r), to(-c_i)]

    vd = jnp.concatenate(vmat(0, 1 + jnp.arange(t)) + vmat(1, t - jnp.arange(t)), axis=1)

    ltr, lti = lam_pow(jnp.full((1,), t))
    rows = [ltr[0, 0], lti[0, 0], ltr[1, 0], lti[1, 0]]
    rows = [r.reshape(S5_OCT, S5_SL) for r in rows] + [jnp.zeros((S5_OCT, S5_SL), F32)] * 4
    lam = jnp.stack(rows, axis=1)
    return kd, wd, vd, lam


def _s5_branch(u_bf, nb, seq, nctx_tok, mats):
    kd, wd, vd, lam = mats
    t = S5_T
    n_lat = nb * seq
    ul = u_bf[:n_lat].reshape(nb, seq // t, t, S5_OCT, LANES).transpose(3, 0, 1, 2, 4)
    uc = u_bf[n_lat:].reshape(nb, nctx_tok // t, t, S5_OCT, LANES).transpose(3, 0, 1, 2, 4)
    nctx = nctx_tok // t
    nc = nctx + seq // t
    u8 = jnp.concatenate([uc, ul], axis=2).reshape(S5_OCT, nb, nc, S5_TL)
    y8 = _s5_scan(u8, kd, wd, vd, lam, nb, nc, nctx).reshape(S5_OCT, nb, nc, t, LANES)
    yc = y8[:, :, :nctx].transpose(1, 2, 3, 0, 4).reshape(nb * nctx_tok, D_S5)
    yl = y8[:, :, nctx:].transpose(1, 2, 3, 0, 4).reshape(n_lat, D_S5)
    return jnp.concatenate([yl, yc], axis=0)


def _hgrn_dir(q_ref, v_ref, f_ref, lbv, o_ref, s_scr, b, d, backward):
    c = HG_C
    x = f_ref[...]
    f = lbv + (1.0 - lbv) * jax.nn.sigmoid(x)
    logf = jnp.log(f)
    kk = (1.0 - lbv) * jax.nn.sigmoid(-x)
    row = lax.broadcasted_iota(jnp.int32, (c, c), 0)
    col = lax.broadcasted_iota(jnp.int32, (c, c), 1)
    keep = (col >= row) if backward else (col <= row)
    tri = keep.astype(BF16)
    cum = jnp.zeros((c, D_HG), F32)
    for piece in _split3(logf):
        cum = cum + jnp.dot(tri, piece, preferred_element_type=F32)
    tot = cum[0:1] if backward else cum[c - 1:c]
    q = q_ref[...]
    v = v_ref[...]
    qe = q * jnp.exp(cum)
    kh = kk * jnp.exp(tot - cum)
    etot = jnp.exp(tot)
    nsub = c // HG_SUB
    for h in range(HG_HEADS):
        sl = slice(h * HG_DK, (h + 1) * HG_DK)
        qh, kh_h, vh, cumh = q[:, sl], kk[:, sl], v[:, sl], cum[:, sl]
        vb16 = vh.astype(BF16)
        st = s_scr[b, d, h]
        inter = lax.dot_general(qe[:, sl].astype(BF16), st.astype(BF16),
                                (((1,), (1,)), ((), ())), preferred_element_type=F32)
        blocks = []
        for i in range(nsub):
            lo, hi = i * HG_SUB, (i + 1) * HG_SUB
            if backward:
                ref = cumh[hi:hi + 1] if i < nsub - 1 else jnp.zeros((1, HG_DK), F32)
            else:
                ref = cumh[lo - 1:lo] if i > 0 else jnp.zeros((1, HG_DK), F32)
            qt = qh[lo:hi] * jnp.exp(cumh[lo:hi] - ref)
            kt = kh_h * jnp.exp(jnp.minimum(ref - cumh, HG_CLAMP))
            blocks.append(lax.dot_general(qt.astype(BF16), kt.astype(BF16),
                                          (((1,), (1,)), ((), ())), preferred_element_type=F32))
        scores = jnp.where(keep, jnp.concatenate(blocks, axis=0), 0.0)
        o_ref[b, :, sl] = inter + jnp.dot(scores.astype(BF16), vb16, preferred_element_type=F32)
        upd = lax.dot_general(vb16, kh[:, sl].astype(BF16),
                              (((0,), (0,)), ((), ())), preferred_element_type=F32)
        s_scr[b, d, h] = st * etot[:, sl] + upd


def _hgrn_kernel(*refs, nb):
    zin = refs[:6 * nb]
    lb_ref, s0_ref, of_ref, ob_ref, sfin_ref, s_scr = refs[6 * nb:]
    i = pl.program_id(0)

    @pl.when(i == 0)
    def _():
        s_scr[...] = s0_ref[...]

    for b in range(nb):
        qf, vf, ff, qb, vb, fb = zin[6 * b:6 * b + 6]
        _hgrn_dir(qf, vf, ff, lb_ref[0:1], of_ref, s_scr, b, 0, False)
        _hgrn_dir(qb, vb, fb, lb_ref[1:2], ob_ref, s_scr, b, 1, True)

    @pl.when(i == pl.num_programs(0) - 1)
    def _():
        sfin_ref[...] = s_scr[...]


def _hgrn_scan(zview, lb2, s0, nb, nsteps, in_f, in_b, out_f, out_b, out_view_shape):
    blk = (HG_C, D_HG)
    oblk = (nb, HG_C, D_HG)
    state_shape = (nb, 2, HG_HEADS, HG_DK, HG_DK)
    state_spec = pl.BlockSpec(state_shape, lambda i: (0, 0, 0, 0, 0))
    in_specs = []
    for b in range(nb):
        in_specs += [pl.BlockSpec(blk, lambda i, b=b, cb=cb: in_f(b, i, cb)) for cb in (1, 2, 3)]
        in_specs += [pl.BlockSpec(blk, lambda i, b=b, cb=cb: in_b(b, i, cb)) for cb in (1, 2, 4)]
    in_specs += [pl.BlockSpec((2, D_HG), lambda i: (0, 0)), state_spec]
    return pl.pallas_call(
        functools.partial(_hgrn_kernel, nb=nb),
        grid=(nsteps,),
        in_specs=in_specs,
        out_specs=[pl.BlockSpec(oblk, lambda i: (0,) + out_f(i)),
                   pl.BlockSpec(oblk, lambda i: (0,) + out_b(i)), state_spec],
        out_shape=[jax.ShapeDtypeStruct(out_view_shape, F32),
                   jax.ShapeDtypeStruct(out_view_shape, F32),
                   jax.ShapeDtypeStruct(state_shape, F32)],
        scratch_shapes=[pltpu.VMEM(state_shape, F32)],
        compiler_params=_cparams(("arbitrary",)),
        name="hgrn_scan",
    )(*([zview] * (6 * nb) + [lb2, s0]))


def _hgrn_branch(z, nb, seq, nctx_tok, lb2):
    n_lat = nb * seq
    c = HG_C
    nsc = nctx_tok // c
    base_c = n_lat // c
    s0 = jnp.zeros((nb, 2, HG_HEADS, HG_DK, HG_DK), F32)
    oc_f, oc_b, s_ctx = _hgrn_scan(
        z, lb2, s0, nb, nsc,
        lambda b, i, cb: (base_c + b * nsc + i, cb),
        lambda b, i, cb: (base_c + b * nsc + (nsc - 1 - i), cb),
        lambda i: (i, 0),
        lambda i: (nsc - 1 - i, 0),
        (nb, nctx_tok, D_HG))
    nsl = seq // c
    ol_f, ol_b, _ = _hgrn_scan(
        z, lb2, s_ctx, nb, nsl,
        lambda b, i, cb: (b * nsl + i, cb),
        lambda b, i, cb: (b * nsl + (nsl - 1 - i), cb),
        lambda i: (i, 0),
        lambda i: (nsl - 1 - i, 0),
        (nb, seq, D_HG))
    return (ol_f.reshape(n_lat, D_HG), ol_b.reshape(n_lat, D_HG),
            oc_f.reshape(nb * nctx_tok, D_HG), oc_b.reshape(nb * nctx_tok, D_HG))


def _gelu_tanh(x):
    return 0.5 * x * (1.0 + jnp.tanh(math.sqrt(2.0 / math.pi) * (x + 0.044715 * (x * x * x))))


def _post_kernel(x_ref, u_ref, g_ref, gs_ref, gh_ref, y5_ref, ofl_ref, obl_ref, ofc_ref, obc_ref,
                 m_ref, d_ref, wglu_ref, hn_ref, wus_ref, wuh_ref, wo_ref, n2_ref,
                 rwh_ref, rwl_ref, rb_ref,
                 xo_ref, h2_ref, gate_ref, meta_ref, cnt_ref, run_scr, *, n_lat_tiles):
    i = pl.program_id(0)

    @pl.when(i == 0)
    def _():
        run_scr[...] = jnp.zeros_like(run_scr)

    m = m_ref[0]
    zz = _gelu_tanh(y5_ref[...] + d_ref[...] * u_ref[...])
    glu = jnp.dot(zz.astype(BF16), wglu_ref[...], preferred_element_type=F32)
    y5 = zz * jax.nn.sigmoid(glu)
    o = jnp.where(i < n_lat_tiles, ofl_ref[...] + obl_ref[...], ofc_ref[...] + obc_ref[...])
    hn = hn_ref[...]
    on = jnp.concatenate(
        [_rms(o[:, h * HG_DK:(h + 1) * HG_DK], hn) for h in range(HG_HEADS)], axis=1)
    g = g_ref[...]
    yh = on * (g * jax.nn.sigmoid(g))
    merged = (jax.nn.sigmoid(gs_ref[...])
              * jnp.dot(y5.astype(BF16), wus_ref[...], preferred_element_type=F32)
              + jax.nn.sigmoid(gh_ref[...])
              * jnp.dot(yh.astype(BF16), wuh_ref[...], preferred_element_type=F32))
    y = jnp.dot(merged.astype(BF16), wo_ref[...], preferred_element_type=F32)
    xn = x_ref[...] + m[:, 2 * D:3 * D] * y
    xo_ref[...] = xn
    h2 = _rms(xn, n2_ref[...]) * (1.0 + m[:, 4 * D:5 * D]) + m[:, 3 * D:4 * D]
    h2_ref[...] = h2

    hi = h2.astype(BF16)
    lo = (h2 - hi.astype(F32)).astype(BF16)
    rwh = rwh_ref[...]
    logits = (jnp.dot(hi, rwh, preferred_element_type=F32)
              + jnp.dot(lo, rwh, preferred_element_type=F32)
              + jnp.dot(hi, rwl_ref[...], preferred_element_type=F32)) + rb_ref[...]
    tm = logits.shape[0]
    lane = lax.broadcasted_iota(jnp.int32, (tm, LANES), 1)
    lane_f = lane.astype(F32)
    l = logits
    vals, idxs, hots = [], [], []
    for _ in range(TOP_K):
        mk = jnp.max(l, axis=1, keepdims=True)
        ik = jnp.min(jnp.where(l == mk, lane_f, float(LANES)), axis=1, keepdims=True)
        hot = lane_f == ik
        vals.append(mk)
        idxs.append(ik)
        hots.append(hot)
        l = jnp.where(hot, NEG * 10.0, l)
    exps = [jnp.exp(vk - vals[0]) for vk in vals]
    den = exps[0] + exps[1] + exps[2] + exps[3]
    sel = (hots[0] | hots[1] | hots[2] | hots[3]).astype(F32)
    r_i = lax.broadcasted_iota(jnp.int32, (tm, tm), 0)
    c_i = lax.broadcasted_iota(jnp.int32, (tm, tm), 1)
    strict = (c_i < r_i).astype(BF16)
    pos = jnp.dot(strict, sel.astype(BF16), preferred_element_type=F32) + run_scr[...]
    run_scr[...] += jnp.sum(sel, axis=0, keepdims=True)
    cnt_ref[...] = run_scr[...]
    gate = jnp.zeros((tm, LANES), F32)
    meta = jnp.zeros((tm, LANES), jnp.int32)
    for k in range(TOP_K):
        pk = jnp.sum(jnp.where(hots[k], pos, 0.0), axis=1, keepdims=True).astype(jnp.int32)
        gate = jnp.where(lane == k, exps[k] / den, gate)
        meta = jnp.where(lane == k, idxs[k].astype(jnp.int32), meta)
        meta = jnp.where(lane == TOP_K + k, pk, meta)
    gate_ref[...] = gate
    meta_ref[...] = meta


def _post(xall, z, y5s, hg_out, mods3, mod_row, p, n_lat_tiles):
    n = xall.shape[0]
    tile = lambda w, cb: pl.BlockSpec((TM, w), lambda i: (i, cb))
    full = lambda a: pl.BlockSpec(a.shape, lambda i: (0,) * a.ndim)
    lat_tile = pl.BlockSpec((TM, D_HG), lambda i: (jnp.minimum(i, n_lat_tiles - 1), 0))
    ctx_tile = pl.BlockSpec((TM, D_HG), lambda i: (jnp.maximum(i - n_lat_tiles, 0), 0))
    consts = [p['d'], p['wglu'], p['hn'], p['wus'], p['wuh'], p['wo'], p['n2'],
              p['rwh'], p['rwl'], p['rb']]
    return pl.pallas_call(
        functools.partial(_post_kernel, n_lat_tiles=n_lat_tiles),
        grid=(n // TM,),
        in_specs=[
            tile(D, 0),
            tile(D_S5, 0),
            tile(D_HG, 5),
            tile(D, 3),
            tile(D, 4),
            tile(D_S5, 0), lat_tile, lat_tile, ctx_tile, ctx_tile,
            pl.BlockSpec((1, 1, N_MOD * D), lambda i: (mod_row(i), 0, 0)),
        ] + [full(a) for a in consts],
        out_specs=[tile(D, 0), tile(D, 0), tile(LANES, 0), tile(LANES, 0),
                   pl.BlockSpec((1, LANES), lambda i: (0, 0))],
        out_shape=[jax.ShapeDtypeStruct((n, D), F32), jax.ShapeDtypeStruct((n, D), F32),
                   jax.ShapeDtypeStruct((n, LANES), F32), jax.ShapeDtypeStruct((n, LANES), jnp.int32),
                   jax.ShapeDtypeStruct((1, LANES), F32)],
        scratch_shapes=[pltpu.VMEM((1, LANES), F32)],
        compiler_params=_cparams(("arbitrary",)),
        name="mixer_post",
    )(xall, z, z, z, z, y5s, *hg_out, mods3, *consts)


def _dispatch_kernel(dest_ref, h_ref, xs_in, xs_ref, sem):
    del xs_in

    def body(r, carry):
        for k in range(TOP_K):
            dst = dest_ref[0, 0, r * TOP_K + k]
            pltpu.make_async_copy(h_ref.at[pl.ds(r, 1)], xs_ref.at[pl.ds(dst, 1)],
                                  sem).start(priority=k % 2)
        return carry

    lax.fori_loop(0, TM, body, 0, unroll=DMA_UNROLL)
    for _ in range(TOP_K):
        pltpu.make_async_copy(h_ref, h_ref, sem).wait()


def _dispatch(h2, dest3, xs0):
    n = h2.shape[0]
    return pl.pallas_call(
        _dispatch_kernel,
        grid=(n // TM,),
        in_specs=[
            pl.BlockSpec((1, 1, TM * TOP_K), lambda i: (i, 0, 0), memory_space=pltpu.SMEM),
            pl.BlockSpec((TM, D), lambda i: (i, 0)),
            pl.BlockSpec(memory_space=pl.ANY),
        ],
        out_specs=pl.BlockSpec(memory_space=pl.ANY),
        out_shape=jax.ShapeDtypeStruct(xs0.shape, F32),
        scratch_shapes=[pltpu.SemaphoreType.DMA(())],
        input_output_aliases={2: 0},
        compiler_params=pltpu.CompilerParams(dimension_semantics=("arbitrary",),
                                             vmem_limit_bytes=VMEM_LIMIT, has_side_effects=True),
        name="moe_dispatch",
    )(dest3, h2, xs0)


def _expert_kernel(be_ref, nu_ref, xs_ref, w1_ref, b1_ref, w2_ref, b2_ref, ys_ref, w1_scr, w2_scr):
    i = pl.program_id(0)
    prev = be_ref[jnp.maximum(i - 1, 0)]

    @pl.when((i == 0) | (be_ref[i] != prev))
    def _():
        rows = 256
        for r in range(0, D, rows):
            w1_scr[r:r + rows, :] = w1_ref[0, r:r + rows, :].astype(BF16)
        for r in range(0, D_FF, rows):
            w2_scr[r:r + rows, :] = w2_ref[0, r:r + rows, :].astype(BF16)

    @pl.when(i < nu_ref[0])
    def _():
        gu = jnp.dot(xs_ref[...].astype(BF16), w1_scr[...], preferred_element_type=F32) + b1_ref[0]
        g = jnp.minimum(gu[:, :D_FF], SWIGLU_LIMIT)
        u = jnp.clip(gu[:, D_FF:], -SWIGLU_LIMIT, SWIGLU_LIMIT)
        a = g * jax.nn.sigmoid(SWIGLU_ALPHA * g) * (u + 1.0)
        ys_ref[...] = jnp.dot(a.astype(BF16), w2_scr[...], preferred_element_type=F32) + b2_ref[0]

    @pl.when(i >= nu_ref[0])
    def _():
        ys_ref[...] = jnp.zeros_like(ys_ref)


def _experts(block_e, n_used, xs, layer, w1, b1, w2, b2):
    n_rows = xs.shape[0]
    grid_spec = pltpu.PrefetchScalarGridSpec(
        num_scalar_prefetch=2,
        grid=(n_rows // MOE_BLK,),
        in_specs=[
            pl.BlockSpec((MOE_BLK, D), lambda i, be, nu: (i, 0)),
            pl.BlockSpec((None, 1, D, 2 * D_FF), lambda i, be, nu: (layer, be[i], 0, 0)),
            pl.BlockSpec((None, 1, 1, 2 * D_FF), lambda i, be, nu: (layer, be[i], 0, 0)),
            pl.BlockSpec((None, 1, D_FF, D), lambda i, be, nu: (layer, be[i], 0, 0)),
            pl.BlockSpec((None, 1, 1, D), lambda i, be, nu: (layer, be[i], 0, 0)),
        ],
        out_specs=pl.BlockSpec((MOE_BLK, D), lambda i, be, nu: (i, 0)),
        scratch_shapes=[pltpu.VMEM((D, 2 * D_FF), BF16), pltpu.VMEM((D_FF, D), BF16)],
    )
    return pl.pallas_call(
        _expert_kernel,
        grid_spec=grid_spec,
        out_shape=jax.ShapeDtypeStruct((n_rows, D), F32),
        compiler_params=_cparams(("arbitrary",)),
        name="moe_experts",
    )(block_e, n_used, xs, w1, b1, w2, b2)


def _combine_kernel(dest_ref, dnext_ref, x_ref, gate_ref, m_ref, ys_ref, xo_ref, buf, sem):
    i = pl.program_id(0)
    n = pl.num_programs(0)
    slot = i % 2

    def start_rows(idx_ref, s):
        def body(r, carry):
            for k in range(TOP_K):
                src = idx_ref[0, 0, r * TOP_K + k]
                pltpu.make_async_copy(ys_ref.at[pl.ds(src, 1)], buf.at[s, k, pl.ds(r, 1)],
                                      sem.at[s]).start(priority=k % 2)
            return carry

        lax.fori_loop(0, TM, body, 0, unroll=DMA_UNROLL)

    @pl.when(i == 0)
    def _():
        start_rows(dest_ref, 0)

    @pl.when(i + 1 < n)
    def _():
        start_rows(dnext_ref, 1 - slot)

    for k in range(TOP_K):
        pltpu.make_async_copy(buf.at[slot, k], buf.at[slot, k], sem.at[slot]).wait()
    gate = gate_ref[...]
    y = gate[:, 0:1] * buf[slot, 0]
    for k in range(1, TOP_K):
        y = y + gate[:, k:k + 1] * buf[slot, k]
    xo_ref[...] = x_ref[...] + m_ref[0][:, 5 * D:6 * D] * y


def _combine(xn, gate, dest3, ys, mods3, mod_row):
    n = xn.shape[0]
    last = n // TM - 1
    return pl.pallas_call(
        _combine_kernel,
        grid=(n // TM,),
        in_specs=[
            pl.BlockSpec((1, 1, TM * TOP_K), lambda i: (i, 0, 0), memory_space=pltpu.SMEM),
            pl.BlockSpec((1, 1, TM * TOP_K), lambda i: (jnp.minimum(i + 1, last), 0, 0),
                         memory_space=pltpu.SMEM),
            pl.BlockSpec((TM, D), lambda i: (i, 0)),
            pl.BlockSpec((TM, LANES), lambda i: (i, 0)),
            pl.BlockSpec((1, 1, N_MOD * D), lambda i: (mod_row(i), 0, 0)),
            pl.BlockSpec(memory_space=pl.ANY),
        ],
        out_specs=pl.BlockSpec((TM, D), lambda i: (i, 0)),
        out_shape=jax.ShapeDtypeStruct((n, D), F32),
        scratch_shapes=[pltpu.VMEM((2, TOP_K, TM, D), F32), pltpu.SemaphoreType.DMA((2,))],
        compiler_params=_cparams(("arbitrary",)),
        name="moe_combine",
    )(dest3, dest3, xn, gate, mods3, ys)


def _final_norm_kernel(x_ref, g_ref, o_ref):
    o_ref[...] = _rms(x_ref[...], g_ref[...])


def _final_norm(xall, g, n_lat):
    return pl.pallas_call(
        _final_norm_kernel,
        grid=(n_lat // TM,),
        in_specs=[pl.BlockSpec((TM, D), lambda i: (i, 0)), pl.BlockSpec((1, D), lambda i: (0, 0))],
        out_specs=pl.BlockSpec((TM, D), lambda i: (i, 0)),
        out_shape=jax.ShapeDtypeStruct((n_lat, D), F32),
        compiler_params=_cparams(("arbitrary",)),
        name="final_norm",
    )(xall, g.reshape(1, D))


def kernel(x, c, ctx, c_ctx, mod_w, mod_b, norm1_g, norm2_g, w_in, s5_a_re, s5_a_im, s5_log_dt,
           s5_b_re, s5_b_im, s5_c_re, s5_c_im, s5_d, s5_w_glu, hgrn_lb_logits, hgrn_norm_g,
           w_up_s5, w_up_hgrn, w_out, router_w, router_b, moe_w1, moe_b1, moe_w2, moe_b2, final_g):
    nb, seq, d = x.shape
    nctx_tok = ctx.shape[1]
    depth = mod_w.shape[0]
    assert d == D and seq % (GRID_W * HG_C) == 0 and nctx_tok % TM == 0 and nb + 1 <= 8
    assert nctx_tok % (8 * S5_T) == 0 and seq % (8 * S5_T) == 0
    n_lat = nb * seq
    n = n_lat + nb * nctx_tok
    tiles_per_batch = seq // TM
    n_lat_tiles = n_lat // TM

    def mod_row(i):
        return jnp.where(i < n_lat_tiles, i // tiles_per_batch, nb)

    cpad = jnp.zeros((8, D), F32).at[:nb].set(c).at[nb].set(c_ctx)
    mods = _modulation(cpad, mod_w, mod_b)
    p_lb = jax.nn.softmax(hgrn_lb_logits.astype(F32), axis=1)
    lb_all = jnp.cumsum(p_lb, axis=1)[:, :depth]

    n_rows = pl.cdiv(n * TOP_K, MOE_BLK) * MOE_BLK + N_EXP * MOE_BLK
    n_blocks = n_rows // MOE_BLK

    rows = seq // GRID_W
    b1_all = moe_b1.reshape(depth, N_EXP, 1, 2 * D_FF)
    b2_all = moe_b2.reshape(depth, N_EXP, 1, D)

    def regrid(xa, to_colmajor):
        shape = (nb, rows, GRID_W, D) if to_colmajor else (nb, GRID_W, rows, D)
        lat = xa[:n_lat].reshape(shape).transpose(0, 2, 1, 3).reshape(n_lat, D)
        return xa.at[:n_lat].set(lat)

    s5_mats = jax.vmap(_s5_matrices)(s5_a_re, s5_a_im, s5_log_dt, s5_b_re, s5_b_im, s5_c_re, s5_c_im)
    w_in_bf, wglu_bf, wus_bf = w_in.astype(BF16), s5_w_glu.astype(BF16), w_up_s5.astype(BF16)
    wuh_bf, wo_bf = w_up_hgrn.astype(BF16), w_out.astype(BF16)
    rw_all = jnp.zeros((depth, D, LANES), F32).at[:, :, :N_EXP].set(router_w)
    rwh_all = rw_all.astype(BF16)
    rwl_all = (rw_all - rwh_all.astype(F32)).astype(BF16)
    rb_all = jnp.full((depth, 1, LANES), NEG, F32).at[:, 0, :N_EXP].set(router_b)

    xall = jnp.concatenate([x.reshape(n_lat, D), ctx.reshape(nb * nctx_tok, D)], axis=0)
    xs = jnp.zeros((n_rows, D), F32)
    colmajor = False
    for l in range(depth):
        if (l % 2 == 1) != colmajor:
            xall = regrid(xall, not colmajor)
            colmajor = not colmajor
        mods3 = mods[l, :nb + 1].reshape(nb + 1, 1, N_MOD * D)
        z, u_bf = _inproj(xall, mods3, norm1_g[l], w_in_bf[l], mod_row)
        y5s = _s5_branch(u_bf, nb, seq, nctx_tok, tuple(m[l] for m in s5_mats))
        hg_out = _hgrn_branch(z, nb, seq, nctx_tok, lb_all[:, l])
        params = dict(d=s5_d[l].reshape(1, D_S5), wglu=wglu_bf[l],
                      hn=hgrn_norm_g[l].reshape(1, HG_DK), wus=wus_bf[l],
                      wuh=wuh_bf[l], wo=wo_bf[l],
                      n2=norm2_g[l].reshape(1, D), rwh=rwh_all[l], rwl=rwl_all[l], rb=rb_all[l])
        xn, h2, gate, meta, cnt = _post(xall, z, y5s, hg_out, mods3, mod_row, params, n_lat_tiles)
        counts = cnt[0, :N_EXP].astype(jnp.int32)
        padded = (counts + MOE_BLK - 1) // MOE_BLK * MOE_BLK
        pad_end = jnp.cumsum(padded)
        pad_start = pad_end - padded
        dest = pad_start[meta[:, :TOP_K]] + meta[:, TOP_K:2 * TOP_K]
        dest3 = dest.reshape(n // TM, 1, TM * TOP_K)
        blk_start = jnp.arange(n_blocks, dtype=jnp.int32) * MOE_BLK
        block_e = jnp.minimum(
            jnp.sum((pad_end[None, :] <= blk_start[:, None]).astype(jnp.int32), axis=1), N_EXP - 1)
        n_used = (pad_end[-1:] // MOE_BLK).astype(jnp.int32)
        xs = _dispatch(h2, dest3, xs)
        ys = _experts(block_e, n_used, xs, l, moe_w1, b1_all, moe_w2, b2_all)
        xall = _combine(xn, gate, dest3, ys, mods3, mod_row)
    if colmajor:
        xall = regrid(xall, False)
    return _final_norm(xall, final_g, n_lat).reshape(nb, seq, D)
```

```python
import functools
import math

import jax
import jax.numpy as jnp
from jax import lax
from jax.experimental import pallas as pl
from jax.experimental.pallas import tpu as pltpu

F32 = jnp.float32
BF16 = jnp.bfloat16

D = 1024
GRID_W = 64
D_S5 = 512
S5_H = 16
S5_G = D_S5 // S5_H
S5_P = 64
D_HG = 512
HG_HEADS = 4
HG_DK = 128
N_IN = D_S5 + 5 * D_HG + 2 * D
N_EXP = 32
TOP_K = 4
D_FF = D
SWIGLU_LIMIT = 7.0
SWIGLU_ALPHA = 1.702
N_MOD = 6
EPS = 1e-6

LANES = 128
TM = 256
S5_T = 16
S5_GB = LANES // S5_H
S5_OCT = S5_G // S5_GB
S5_TL = S5_T * LANES
S5_SL = S5_GB * S5_P
HG_C = 128
HG_SUB = 16
HG_CLAMP = 80.0
MOE_BLK = 256
DMA_UNROLL = 4
VMEM_LIMIT = 56 * 1024 * 1024
NEG = -1e30


def _cparams(sem, vmem=VMEM_LIMIT):
    return pltpu.CompilerParams(dimension_semantics=sem, vmem_limit_bytes=vmem)


def _rms(x, g):
    return x * lax.rsqrt(jnp.mean(x * x, axis=-1, keepdims=True) + EPS) * g


def _split3(x):
    hi = x.astype(BF16)
    r1 = x - hi.astype(F32)
    mid = r1.astype(BF16)
    lo = (r1 - mid.astype(F32)).astype(BF16)
    return hi, mid, lo


def _mod_kernel(c_ref, w_ref, b_ref, o_ref):
    c = c_ref[...]
    sc = c * jax.nn.sigmoid(c)
    acc = jnp.zeros(o_ref.shape[1:], F32)
    w = w_ref[0]
    for piece in _split3(sc):
        for wp in _split3(w):
            acc = acc + jnp.dot(piece, wp, preferred_element_type=F32)
    o_ref[0] = acc + b_ref[0]


def _modulation(cpad, mod_w, mod_b):
    depth = mod_w.shape[0]
    tn = 1536
    return pl.pallas_call(
        _mod_kernel,
        grid=(depth, (N_MOD * D) // tn),
        in_specs=[
            pl.BlockSpec((8, D), lambda l, j: (0, 0)),
            pl.BlockSpec((1, D, tn), lambda l, j: (l, 0, j)),
            pl.BlockSpec((1, 1, tn), lambda l, j: (l, 0, j)),
        ],
        out_specs=pl.BlockSpec((1, 8, tn), lambda l, j: (l, 0, j)),
        out_shape=jax.ShapeDtypeStruct((depth, 8, N_MOD * D), F32),
        compiler_params=_cparams(("arbitrary", "arbitrary")),
        name="modulation",
    )(cpad, mod_w, mod_b.reshape(depth, 1, N_MOD * D))


def _inproj_kernel(x_ref, m_ref, g_ref, w_ref, z_ref, u_ref):
    m = m_ref[0]
    h = _rms(x_ref[...], g_ref[...])
    h = h * (1.0 + m[:, D:2 * D]) + m[:, 0:D]
    z = jnp.dot(h.astype(BF16), w_ref[...], preferred_element_type=F32)
    z_ref[...] = z
    u_ref[...] = z[:, :D_S5].astype(BF16)


def _inproj(xall, mods3, g, w_bf, mod_row):
    n = xall.shape[0]
    return pl.pallas_call(
        _inproj_kernel,
        grid=(n // TM,),
        in_specs=[
            pl.BlockSpec((TM, D), lambda i: (i, 0)),
            pl.BlockSpec((1, 1, N_MOD * D), lambda i: (mod_row(i), 0, 0)),
            pl.BlockSpec((1, D), lambda i: (0, 0)),
            pl.BlockSpec((D, N_IN), lambda i: (0, 0)),
        ],
        out_specs=[pl.BlockSpec((TM, N_IN), lambda i: (i, 0)),
                   pl.BlockSpec((TM, D_S5), lambda i: (i, 0))],
        out_shape=[jax.ShapeDtypeStruct((n, N_IN), F32), jax.ShapeDtypeStruct((n, D_S5), BF16)],
        compiler_params=_cparams(("arbitrary",)),
        name="inproj",
    )(xall, mods3, g.reshape(1, D), w_bf)


def _s5_row_block(nc):
    return max(r for r in range(16, 257, 16) if nc % r == 0)


def _s5_expand(kd_ref, wd_ref, vd_ref, mcat_scr, v_scr):
    t = S5_T

    def rep(n_in, n_out):
        r = lax.broadcasted_iota(jnp.int32, (n_in, n_out), 0)
        c = lax.broadcasted_iota(jnp.int32, (n_in, n_out), 1)
        return (c % n_in == r).astype(BF16)

    def group_mask(rows, cols, rdiv, cdiv):
        r = lax.broadcasted_iota(jnp.int32, (rows, cols), 0)
        c = lax.broadcasted_iota(jnp.int32, (rows, cols), 1)
        return r // rdiv == c // cdiv

    def widen(compact, r01, mask):
        full = lax.dot_general(compact.astype(BF16), r01, (((0,), (0,)), ((), ())),
                               preferred_element_type=F32)
        return jnp.where(mask, full, 0.0).astype(BF16)

    r_h, r_p = rep(S5_H, LANES), rep(S5_P, S5_SL)
    m_mask = group_mask(LANES, LANES, S5_H, S5_H)
    w_mask = group_mask(LANES, S5_SL, S5_H, S5_P)
    v_mask = group_mask(S5_SL, LANES, S5_P, S5_H)
    for lag in range(2 * t - 1):
        tile = widen(kd_ref[0, lag], r_h, m_mask)
        for s in range(t):
            tt = s + lag - (t - 1)
            if 0 <= tt < t:
                mcat_scr[s * LANES:(s + 1) * LANES, tt * LANES:(tt + 1) * LANES] = tile
    for kind in range(4):
        for s in range(t):
            c0 = S5_TL + kind * S5_SL
            mcat_scr[s * LANES:(s + 1) * LANES, c0:c0 + S5_SL] = widen(
                wd_ref[0, kind * t + s], r_p, w_mask)
            v_scr[kind * S5_SL:(kind + 1) * S5_SL, s * LANES:(s + 1) * LANES] = widen(
                vd_ref[0, kind * t + s], r_h, v_mask)


def _s5_kernel(u_ref, kd_ref, wd_ref, vd_ref, lam_ref, y_ref, d_scr, mcat_scr, v_scr, *, nc, nctx):
    @pl.when(pl.program_id(1) == 0)
    def _():
        _s5_expand(kd_ref, wd_ref, vd_ref, mcat_scr, v_scr)

    rb = _s5_row_block(nc)
    for r0 in range(0, nc, rb):
        ub = u_ref[0, 0, r0:r0 + rb, :]
        y_ref[0, 0, r0:r0 + rb, :] = jnp.dot(ub, mcat_scr[:, :S5_TL], preferred_element_type=F32)
        d_scr[r0:r0 + rb, :] = jnp.dot(ub, mcat_scr[:, S5_TL:], preferred_element_type=F32)
    lam = lam_ref[0]
    sub = 8
    zero = jnp.zeros((sub, S5_SL), F32)
    srow = lax.broadcasted_iota(jnp.int32, (sub, S5_SL), 0)
    ntile, nctile = nc // sub, nctx // sub

    def bcast(row):
        return jnp.broadcast_to(row, (sub, S5_SL))

    def run(col, lr, li, tile_of, descending):
        def body(j, carry):
            xr, xi = carry
            base = pl.multiple_of(tile_of(j) * sub, sub)
            dt = d_scr[pl.ds(base, sub), col:col + 2 * S5_SL]
            accr, acci = zero, zero
            for step in range(sub):
                s = sub - 1 - step if descending else step
                accr = jnp.where(srow == s, xr, accr)
                acci = jnp.where(srow == s, xi, acci)
                dr = bcast(dt[s:s + 1, 0:S5_SL])
                di = bcast(dt[s:s + 1, S5_SL:2 * S5_SL])
                xr, xi = lr * xr - li * xi + dr, lr * xi + li * xr + di
            d_scr[pl.ds(base, sub), col:col + S5_SL] = accr
            d_scr[pl.ds(base, sub), col + S5_SL:col + 2 * S5_SL] = acci
            return xr, xi

        lax.fori_loop(0, ntile, body, (zero, zero))

    run(0, lam[0:1], lam[1:2], lambda j: j, False)
    run(2 * S5_SL, lam[2:3], lam[3:4],
        lambda j: jnp.where(j < nctile, nctile - 1 - j, ntile - 1 - (j - nctile)), True)
    for r0 in range(0, nc, rb):
        y_ref[0, 0, r0:r0 + rb, :] += jnp.dot(d_scr[r0:r0 + rb, :].astype(BF16), v_scr[...],
                                               preferred_element_type=F32)


def _s5_scan(u8, kd, wd, vd, lam, nb, nc, nctx):
    kern = functools.partial(_s5_kernel, nc=nc, nctx=nctx)
    once = pl.Buffered(1)
    t = S5_T
    return pl.pallas_call(
        kern,
        grid=(S5_OCT, nb),
        in_specs=[
            pl.BlockSpec((1, 1, nc, S5_TL), lambda o, b: (o, b, 0, 0)),
            pl.BlockSpec((1, 2 * t - 1, S5_H, LANES), lambda o, b: (o, 0, 0, 0), pipeline_mode=once),
            pl.BlockSpec((1, 4 * t, S5_P, LANES), lambda o, b: (o, 0, 0, 0), pipeline_mode=once),
            pl.BlockSpec((1, 4 * t, S5_H, S5_SL), lambda o, b: (o, 0, 0, 0), pipeline_mode=once),
            pl.BlockSpec((1, 8, S5_SL), lambda o, b: (o, 0, 0)),
        ],
        out_specs=pl.BlockSpec((1, 1, nc, S5_TL), lambda o, b: (o, b, 0, 0), pipeline_mode=once),
        out_shape=jax.ShapeDtypeStruct((S5_OCT, nb, nc, S5_TL), F32),
        scratch_shapes=[pltpu.VMEM((nc, 4 * S5_SL), F32),
                        pltpu.VMEM((S5_TL, S5_TL + 4 * S5_SL), BF16),
                        pltpu.VMEM((4 * S5_SL, S5_TL), BF16)],
        compiler_params=_cparams(("arbitrary", "arbitrary")),
        name="s5_scan",
    )(u8, kd, wd, vd, lam)


def _s5_matrices(a_re, a_im, log_dt, b_re, b_im, c_re, c_im):
    t = S5_T
    dt = jnp.exp(log_dt.astype(F32))[..., None]
    are, aim = a_re.astype(F32), a_im.astype(F32)

    def lam_pow(tau):
        tau = tau.astype(F32)[None, :, None, None]
        mag = jnp.exp(are[:, None] * dt[:, None] * tau)
        ang = aim[:, None] * dt[:, None] * tau
        return mag * jnp.cos(ang), mag * jnp.sin(ang)

    mag1 = jnp.exp(are * dt)
    lbr, lbi = mag1 * jnp.cos(aim * dt), mag1 * jnp.sin(aim * dt)
    den = are * are + aim * aim
    nr = lbr - 1.0
    qr = ((nr * are + lbi * aim) / den)[..., None]
    qi = ((lbi * are - nr * aim) / den)[..., None]
    bre, bim = b_re.astype(F32), b_im.astype(F32)
    bbr = qr * bre - qi * bim
    bbi = qr * bim + qi * bre
    cre, cim = c_re.astype(F32), c_im.astype(F32)

    def c_lam(d, p_r, p_i):
        return (cre[d][None] * p_r[:, :, None, :] - cim[d][None] * p_i[:, :, None, :],
                cre[d][None] * p_i[:, :, None, :] + cim[d][None] * p_r[:, :, None, :])

    def lag_kernel(d, tau):
        p_r, p_i = lam_pow(tau)
        c_r, c_i = c_lam(d, p_r[d], p_i[d])
        return (jnp.einsum('tgop,gpi->tgoi', c_r, bbr[d]) - jnp.einsum('tgop,gpi->tgoi', c_i, bbi[d]))

    def octet(a, g_axis):
        return a.reshape(a.shape[:g_axis] + (S5_OCT, S5_GB) + a.shape[g_axis + 1:])

    kf = lag_kernel(0, jnp.arange(t))
    kb0 = lag_kernel(1, jnp.arange(1))
    kb = lag_kernel(1, t - 1 - jnp.arange(t - 1))
    kall = jnp.concatenate([kb, kf[:1] + kb0, kf[1:]], axis=0)
    kd = octet(kall, 1).transpose(1, 0, 3, 2, 4).reshape(S5_OCT, 2 * t - 1, S5_H, LANES)

    def wmat(d, tau):
        p_r, p_i = lam_pow(tau)
        p_r, p_i = p_r[d][:, :, :, None], p_i[d][:, :, :, None]
        wr = p_r * bbr[d][None] - p_i * bbi[d][None]
        wi = p_r * bbi[d][None] + p_i * bbr[d][None]
        to = lambda w: octet(w, 1).transpose(1, 0, 3, 2, 4).reshape(S5_OCT, t, S5_P, LANES)
        return [to(wr), to(wi)]

    wd = jnp.concatenate(wmat(0, t - 1 - jnp.arange(t)) + wmat(1, jnp.arange(t)), axis=1)

    def vmat(d, tau):
        p_r, p_i = lam_pow(tau)
        c_r, c_i = c_lam(d, p_r[d], p_i[d])
        to = lambda c: octet(c, 1).transpose(1, 0, 3, 2, 4).reshape(S5_OCT, t, S5_H, S5_SL)
        return [to(c_r), to(-c_i)]

    vd = jnp.concatenate(vmat(0, 1 + jnp.arange(t)) + vmat(1, t - jnp.arange(t)), axis=1)

    ltr, lti = lam_pow(jnp.full((1,), t))
    rows = [ltr[0, 0], lti[0, 0], ltr[1, 0], lti[1, 0]]
    rows = [r.reshape(S5_OCT, S5_SL) for r in rows] + [jnp.zeros((S5_OCT, S5_SL), F32)] * 4
    lam = jnp.stack(rows, axis=1)
    return kd, wd, vd, lam


def _s5_branch(u_bf, nb, seq, nctx_tok, mats):
    kd, wd, vd, lam = mats
    t = S5_T
    n_lat = nb * seq
    ul = u_bf[:n_lat].reshape(nb, seq // t, t, S5_OCT, LANES).transpose(3, 0, 1, 2, 4)
    uc = u_bf[n_lat:].reshape(nb, nctx_tok // t, t, S5_OCT, LANES).transpose(3, 0, 1, 2, 4)
    nctx = nctx_tok // t
    nc = nctx + seq // t
    u8 = jnp.concatenate([uc, ul], axis=2).reshape(S5_OCT, nb, nc, S5_TL)
    y8 = _s5_scan(u8, kd, wd, vd, lam, nb, nc, nctx).reshape(S5_OCT, nb, nc, t, LANES)
    yc = y8[:, :, :nctx].transpose(1, 2, 3, 0, 4).reshape(nb * nctx_tok, D_S5)
    yl = y8[:, :, nctx:].transpose(1, 2, 3, 0, 4).reshape(n_lat, D_S5)
    return jnp.concatenate([yl, yc], axis=0)


def _hgrn_dir(q_ref, v_ref, f_ref, lbv, o_ref, s_scr, b, d, backward):
    c = HG_C
    x = f_ref[...]
    f = lbv + (1.0 - lbv) * jax.nn.sigmoid(x)
    logf = jnp.log(f)
    kk = (1.0 - lbv) * jax.nn.sigmoid(-x)
    row = lax.broadcasted_iota(jnp.int32, (c, c), 0)
    col = lax.broadcasted_iota(jnp.int32, (c, c), 1)
    keep = (col >= row) if backward else (col <= row)
    tri = keep.astype(BF16)
    cum = jnp.zeros((c, D_HG), F32)
    for piece in _split3(logf):
        cum = cum + jnp.dot(tri, piece, preferred_element_type=F32)
    tot = cum[0:1] if backward else cum[c - 1:c]
    q = q_ref[...]
    v = v_ref[...]
    qe = q * jnp.exp(cum)
    kh = kk * jnp.exp(tot - cum)
    etot = jnp.exp(tot)
    nsub = c // HG_SUB
    for h in range(HG_HEADS):
        sl = slice(h * HG_DK, (h + 1) * HG_DK)
        qh, kh_h, vh, cumh = q[:, sl], kk[:, sl], v[:, sl], cum[:, sl]
        vb16 = vh.astype(BF16)
        st = s_scr[b, d, h]
        inter = lax.dot_general(qe[:, sl].astype(BF16), st.astype(BF16),
                                (((1,), (1,)), ((), ())), preferred_element_type=F32)
        blocks = []
        for i in range(nsub):
            lo, hi = i * HG_SUB, (i + 1) * HG_SUB
            if backward:
                ref = cumh[hi:hi + 1] if i < nsub - 1 else jnp.zeros((1, HG_DK), F32)
            else:
                ref = cumh[lo - 1:lo] if i > 0 else jnp.zeros((1, HG_DK), F32)
            qt = qh[lo:hi] * jnp.exp(cumh[lo:hi] - ref)
            kt = kh_h * jnp.exp(jnp.minimum(ref - cumh, HG_CLAMP))
            blocks.append(lax.dot_general(qt.astype(BF16), kt.astype(BF16),
                                          (((1,), (1,)), ((), ())), preferred_element_type=F32))
        scores = jnp.where(keep, jnp.concatenate(blocks, axis=0), 0.0)
        o_ref[b, :, sl] = inter + jnp.dot(scores.astype(BF16), vb16, preferred_element_type=F32)
        upd = lax.dot_general(vb16, kh[:, sl].astype(BF16),
                              (((0,), (0,)), ((), ())), preferred_element_type=F32)
        s_scr[b, d, h] = st * etot[:, sl] + upd


def _hgrn_kernel(*refs, nb):
    zin = refs[:6 * nb]
    lb_ref, s0_ref, of_ref, ob_ref, sfin_ref, s_scr = refs[6 * nb:]
    i = pl.program_id(0)

    @pl.when(i == 0)
    def _():
        s_scr[...] = s0_ref[...]

    for b in range(nb):
        qf, vf, ff, qb, vb, fb = zin[6 * b:6 * b + 6]
        _hgrn_dir(qf, vf, ff, lb_ref[0:1], of_ref, s_scr, b, 0, False)
        _hgrn_dir(qb, vb, fb, lb_ref[1:2], ob_ref, s_scr, b, 1, True)

    @pl.when(i == pl.num_programs(0) - 1)
    def _():
        sfin_ref[...] = s_scr[...]


def _hgrn_scan(zview, lb2, s0, nb, nsteps, in_f, in_b, out_f, out_b, out_view_shape):
    blk = (HG_C, D_HG)
    oblk = (nb, HG_C, D_HG)
    state_shape = (nb, 2, HG_HEADS, HG_DK, HG_DK)
    state_spec = pl.BlockSpec(state_shape, lambda i: (0, 0, 0, 0, 0))
    in_specs = []
    for b in range(nb):
        in_specs += [pl.BlockSpec(blk, lambda i, b=b, cb=cb: in_f(b, i, cb)) for cb in (1, 2, 3)]
        in_specs += [pl.BlockSpec(blk, lambda i, b=b, cb=cb: in_b(b, i, cb)) for cb in (1, 2, 4)]
    in_specs += [pl.BlockSpec((2, D_HG), lambda i: (0, 0)), state_spec]
    return pl.pallas_call(
        functools.partial(_hgrn_kernel, nb=nb),
        grid=(nsteps,),
        in_specs=in_specs,
        out_specs=[pl.BlockSpec(oblk, lambda i: (0,) + out_f(i)),
                   pl.BlockSpec(oblk, lambda i: (0,) + out_b(i)), state_spec],
        out_shape=[jax.ShapeDtypeStruct(out_view_shape, F32),
                   jax.ShapeDtypeStruct(out_view_shape, F32),
                   jax.ShapeDtypeStruct(state_shape, F32)],
        scratch_shapes=[pltpu.VMEM(state_shape, F32)],
        compiler_params=_cparams(("arbitrary",)),
        name="hgrn_scan",
    )(*([zview] * (6 * nb) + [lb2, s0]))


def _hgrn_branch(z, nb, seq, nctx_tok, lb2):
    n_lat = nb * seq
    c = HG_C
    nsc = nctx_tok // c
    base_c = n_lat // c
    s0 = jnp.zeros((nb, 2, HG_HEADS, HG_DK, HG_DK), F32)
    oc_f, oc_b, s_ctx = _hgrn_scan(
        z, lb2, s0, nb, nsc,
        lambda b, i, cb: (base_c + b * nsc + i, cb),
        lambda b, i, cb: (base_c + b * nsc + (nsc - 1 - i), cb),
        lambda i: (i, 0),
        lambda i: (nsc - 1 - i, 0),
        (nb, nctx_tok, D_HG))
    nsl = seq // c
    ol_f, ol_b, _ = _hgrn_scan(
        z, lb2, s_ctx, nb, nsl,
        lambda b, i, cb: (b * nsl + i, cb),
        lambda b, i, cb: (b * nsl + (nsl - 1 - i), cb),
        lambda i: (i, 0),
        lambda i: (nsl - 1 - i, 0),
        (nb, seq, D_HG))
    return (ol_f.reshape(n_lat, D_HG), ol_b.reshape(n_lat, D_HG),
            oc_f.reshape(nb * nctx_tok, D_HG), oc_b.reshape(nb * nctx_tok, D_HG))


def _gelu_tanh(x):
    return 0.5 * x * (1.0 + jnp.tanh(math.sqrt(2.0 / math.pi) * (x + 0.044715 * (x * x * x))))


def _post_kernel(x_ref, u_ref, g_ref, gs_ref, gh_ref, y5_ref, ofl_ref, obl_ref, ofc_ref, obc_ref,
                 m_ref, d_ref, wglu_ref, hn_ref, wus_ref, wuh_ref, wo_ref, n2_ref,
                 rwh_ref, rwl_ref, rb_ref,
                 xo_ref, h2_ref, gate_ref, meta_ref, cnt_ref, run_scr, *, n_lat_tiles):
    i = pl.program_id(0)

    @pl.when(i == 0)
    def _():
        run_scr[...] = jnp.zeros_like(run_scr)

    m = m_ref[0]
    zz = _gelu_tanh(y5_ref[...] + d_ref[...] * u_ref[...])
    glu = jnp.dot(zz.astype(BF16), wglu_ref[...], preferred_element_type=F32)
    y5 = zz * jax.nn.sigmoid(glu)
    o = jnp.where(i < n_lat_tiles, ofl_ref[...] + obl_ref[...], ofc_ref[...] + obc_ref[...])
    hn = hn_ref[...]
    on = jnp.concatenate(
        [_rms(o[:, h * HG_DK:(h + 1) * HG_DK], hn) for h in range(HG_HEADS)], axis=1)
    g = g_ref[...]
    yh = on * (g * jax.nn.sigmoid(g))
    merged = (jax.nn.sigmoid(gs_ref[...])
              * jnp.dot(y5.astype(BF16), wus_ref[...], preferred_element_type=F32)
              + jax.nn.sigmoid(gh_ref[...])
              * jnp.dot(yh.astype(BF16), wuh_ref[...], preferred_element_type=F32))
    y = jnp.dot(merged.astype(BF16), wo_ref[...], preferred_element_type=F32)
    xn = x_ref[...] + m[:, 2 * D:3 * D] * y
    xo_ref[...] = xn
    h2 = _rms(xn, n2_ref[...]) * (1.0 + m[:, 4 * D:5 * D]) + m[:, 3 * D:4 * D]
    h2_ref[...] = h2

    hi = h2.astype(BF16)
    lo = (h2 - hi.astype(F32)).astype(BF16)
    rwh = rwh_ref[...]
    logits = (jnp.dot(hi, rwh, preferred_element_type=F32)
              + jnp.dot(lo, rwh, preferred_element_type=F32)
              + jnp.dot(hi, rwl_ref[...], preferred_element_type=F32)) + rb_ref[...]
    tm = logits.shape[0]
    lane = lax.broadcasted_iota(jnp.int32, (tm, LANES), 1)
    lane_f = lane.astype(F32)
    l = logits
    vals, idxs, hots = [], [], []
    for _ in range(TOP_K):
        mk = jnp.max(l, axis=1, keepdims=True)
        ik = jnp.min(jnp.where(l == mk, lane_f, float(LANES)), axis=1, keepdims=True)
        hot = lane_f == ik
        vals.append(mk)
        idxs.append(ik)
        hots.append(hot)
        l = jnp.where(hot, NEG * 10.0, l)
    exps = [jnp.exp(vk - vals[0]) for vk in vals]
    den = exps[0] + exps[1] + exps[2] + exps[3]
    sel = (hots[0] | hots[1] | hots[2] | hots[3]).astype(F32)
    r_i = lax.broadcasted_iota(jnp.int32, (tm, tm), 0)
    c_i = lax.broadcasted_iota(jnp.int32, (tm, tm), 1)
    strict = (c_i < r_i).astype(BF16)
    pos = jnp.dot(strict, sel.astype(BF16), preferred_element_type=F32) + run_scr[...]
    run_scr[...] += jnp.sum(sel, axis=0, keepdims=True)
    cnt_ref[...] = run_scr[...]
    gate = jnp.zeros((tm, LANES), F32)
    meta = jnp.zeros((tm, LANES), jnp.int32)
    for k in range(TOP_K):
        pk = jnp.sum(jnp.where(hots[k], pos, 0.0), axis=1, keepdims=True).astype(jnp.int32)
        gate = jnp.where(lane == k, exps[k] / den, gate)
        meta = jnp.where(lane == k, idxs[k].astype(jnp.int32), meta)
        meta = jnp.where(lane == TOP_K + k, pk, meta)
    gate_ref[...] = gate
    meta_ref[...] = meta


def _post(xall, z, y5s, hg_out, mods3, mod_row, p, n_lat_tiles):
    n = xall.shape[0]
    tile = lambda w, cb: pl.BlockSpec((TM, w), lambda i: (i, cb))
    full = lambda a: pl.BlockSpec(a.shape, lambda i: (0,) * a.ndim)
    lat_tile = pl.BlockSpec((TM, D_HG), lambda i: (jnp.minimum(i, n_lat_tiles - 1), 0))
    ctx_tile = pl.BlockSpec((TM, D_HG), lambda i: (jnp.maximum(i - n_lat_tiles, 0), 0))
    consts = [p['d'], p['wglu'], p['hn'], p['wus'], p['wuh'], p['wo'], p['n2'],
              p['rwh'], p['rwl'], p['rb']]
    return pl.pallas_call(
        functools.partial(_post_kernel, n_lat_tiles=n_lat_tiles),
        grid=(n // TM,),
        in_specs=[
            tile(D, 0),
            tile(D_S5, 0),
            tile(D_HG, 5),
            tile(D, 3),
            tile(D, 4),
            tile(D_S5, 0), lat_tile, lat_tile, ctx_tile, ctx_tile,
            pl.BlockSpec((1, 1, N_MOD * D), lambda i: (mod_row(i), 0, 0)),
        ] + [full(a) for a in consts],
        out_specs=[tile(D, 0), tile(D, 0), tile(LANES, 0), tile(LANES, 0),
                   pl.BlockSpec((1, LANES), lambda i: (0, 0))],
        out_shape=[jax.ShapeDtypeStruct((n, D), F32), jax.ShapeDtypeStruct((n, D), F32),
                   jax.ShapeDtypeStruct((n, LANES), F32), jax.ShapeDtypeStruct((n, LANES), jnp.int32),
                   jax.ShapeDtypeStruct((1, LANES), F32)],
        scratch_shapes=[pltpu.VMEM((1, LANES), F32)],
        compiler_params=_cparams(("arbitrary",)),
        name="mixer_post",
    )(xall, z, z, z, z, y5s, *hg_out, mods3, *consts)


def _dispatch_kernel(dest_ref, h_ref, xs_in, xs_ref, sem):
    del xs_in

    def body(r, carry):
        for k in range(TOP_K):
            dst = dest_ref[0, 0, r * TOP_K + k]
            pltpu.make_async_copy(h_ref.at[pl.ds(r, 1)], xs_ref.at[pl.ds(dst, 1)],
                                  sem).start(priority=k % 2)
        return carry

    lax.fori_loop(0, TM, body, 0, unroll=DMA_UNROLL)
    for _ in range(TOP_K):
        pltpu.make_async_copy(h_ref, h_ref, sem).wait()


def _dispatch(h2, dest3, xs0):
    n = h2.shape[0]
    return pl.pallas_call(
        _dispatch_kernel,
        grid=(n // TM,),
        in_specs=[
            pl.BlockSpec((1, 1, TM * TOP_K), lambda i: (i, 0, 0), memory_space=pltpu.SMEM),
            pl.BlockSpec((TM, D), lambda i: (i, 0)),
            pl.BlockSpec(memory_space=pl.ANY),
        ],
        out_specs=pl.BlockSpec(memory_space=pl.ANY),
        out_shape=jax.ShapeDtypeStruct(xs0.shape, F32),
        scratch_shapes=[pltpu.SemaphoreType.DMA(())],
        input_output_aliases={2: 0},
        compiler_params=pltpu.CompilerParams(dimension_semantics=("arbitrary",),
                                             vmem_limit_bytes=VMEM_LIMIT, has_side_effects=True),
        name="moe_dispatch",
    )(dest3, h2, xs0)


def _expert_kernel(be_ref, nu_ref, xs_ref, w1_ref, b1_ref, w2_ref, b2_ref, ys_ref, w1_scr, w2_scr):
    i = pl.program_id(0)
    prev = be_ref[jnp.maximum(i - 1, 0)]

    @pl.when((i == 0) | (be_ref[i] != prev))
    def _():
        rows = 256
        for r in range(0, D, rows):
            w1_scr[r:r + rows, :] = w1_ref[0, r:r + rows, :].astype(BF16)
        for r in range(0, D_FF, rows):
            w2_scr[r:r + rows, :] = w2_ref[0, r:r + rows, :].astype(BF16)

    @pl.when(i < nu_ref[0])
    def _():
        gu = jnp.dot(xs_ref[...].astype(BF16), w1_scr[...], preferred_element_type=F32) + b1_ref[0]
        g = jnp.minimum(gu[:, :D_FF], SWIGLU_LIMIT)
        u = jnp.clip(gu[:, D_FF:], -SWIGLU_LIMIT, SWIGLU_LIMIT)
        a = g * jax.nn.sigmoid(SWIGLU_ALPHA * g) * (u + 1.0)
        ys_ref[...] = jnp.dot(a.astype(BF16), w2_scr[...], preferred_element_type=F32) + b2_ref[0]

    @pl.when(i >= nu_ref[0])
    def _():
        ys_ref[...] = jnp.zeros_like(ys_ref)


def _experts(block_e, n_used, xs, layer, w1, b1, w2, b2):
    n_rows = xs.shape[0]
    grid_spec = pltpu.PrefetchScalarGridSpec(
        num_scalar_prefetch=2,
        grid=(n_rows // MOE_BLK,),
        in_specs=[
            pl.BlockSpec((MOE_BLK, D), lambda i, be, nu: (i, 0)),
            pl.BlockSpec((None, 1, D, 2 * D_FF), lambda i, be, nu: (layer, be[i], 0, 0)),
            pl.BlockSpec((None, 1, 1, 2 * D_FF), lambda i, be, nu: (layer, be[i], 0, 0)),
            pl.BlockSpec((None, 1, D_FF, D), lambda i, be, nu: (layer, be[i], 0, 0)),
            pl.BlockSpec((None, 1, 1, D), lambda i, be, nu: (layer, be[i], 0, 0)),
        ],
        out_specs=pl.BlockSpec((MOE_BLK, D), lambda i, be, nu: (i, 0)),
        scratch_shapes=[pltpu.VMEM((D, 2 * D_FF), BF16), pltpu.VMEM((D_FF, D), BF16)],
    )
    return pl.pallas_call(
        _expert_kernel,
        grid_spec=grid_spec,
        out_shape=jax.ShapeDtypeStruct((n_rows, D), F32),
        compiler_params=_cparams(("arbitrary",)),
        name="moe_experts",
    )(block_e, n_used, xs, w1, b1, w2, b2)


def _combine_kernel(dest_ref, dnext_ref, x_ref, gate_ref, m_ref, ys_ref, xo_ref, buf, sem):
    i = pl.program_id(0)
    n = pl.num_programs(0)
    slot = i % 2

    def start_rows(idx_ref, s):
        def body(r, carry):
            for k in range(TOP_K):
                src = idx_ref[0, 0, r * TOP_K + k]
                pltpu.make_async_copy(ys_ref.at[pl.ds(src, 1)], buf.at[s, k, pl.ds(r, 1)],
                                      sem.at[s]).start(priority=k % 2)
            return carry

        lax.fori_loop(0, TM, body, 0, unroll=DMA_UNROLL)

    @pl.when(i == 0)
    def _():
        start_rows(dest_ref, 0)

    @pl.when(i + 1 < n)
    def _():
        start_rows(dnext_ref, 1 - slot)

    for k in range(TOP_K):
        pltpu.make_async_copy(buf.at[slot, k], buf.at[slot, k], sem.at[slot]).wait()
    gate = gate_ref[...]
    y = gate[:, 0:1] * buf[slot, 0]
    for k in range(1, TOP_K):
        y = y + gate[:, k:k + 1] * buf[slot, k]
    xo_ref[...] = x_ref[...] + m_ref[0][:, 5 * D:6 * D] * y


def _combine(xn, gate, dest3, ys, mods3, mod_row):
    n = xn.shape[0]
    last = n // TM - 1
    return pl.pallas_call(
        _combine_kernel,
        grid=(n // TM,),
        in_specs=[
            pl.BlockSpec((1, 1, TM * TOP_K), lambda i: (i, 0, 0), memory_space=pltpu.SMEM),
            pl.BlockSpec((1, 1, TM * TOP_K), lambda i: (jnp.minimum(i + 1, last), 0, 0),
                         memory_space=pltpu.SMEM),
            pl.BlockSpec((TM, D), lambda i: (i, 0)),
            pl.BlockSpec((TM, LANES), lambda i: (i, 0)),
            pl.BlockSpec((1, 1, N_MOD * D), lambda i: (mod_row(i), 0, 0)),
            pl.BlockSpec(memory_space=pl.ANY),
        ],
        out_specs=pl.BlockSpec((TM, D), lambda i: (i, 0)),
        out_shape=jax.ShapeDtypeStruct((n, D), F32),
        scratch_shapes=[pltpu.VMEM((2, TOP_K, TM, D), F32), pltpu.SemaphoreType.DMA((2,))],
        compiler_params=_cparams(("arbitrary",)),
        name="moe_combine",
    )(dest3, dest3, xn, gate, mods3, ys)


def _final_norm_kernel(x_ref, g_ref, o_ref):
    o_ref[...] = _rms(x_ref[...], g_ref[...])


def _final_norm(xall, g, n_lat):
    return pl.pallas_call(
        _final_norm_kernel,
        grid=(n_lat // TM,),
        in_specs=[pl.BlockSpec((TM, D), lambda i: (i, 0)), pl.BlockSpec((1, D), lambda i: (0, 0))],
        out_specs=pl.BlockSpec((TM, D), lambda i: (i, 0)),
        out_shape=jax.ShapeDtypeStruct((n_lat, D), F32),
        compiler_params=_cparams(("arbitrary",)),
        name="final_norm",
    )(xall, g.reshape(1, D))


def kernel(x, c, ctx, c_ctx, mod_w, mod_b, norm1_g, norm2_g, w_in, s5_a_re, s5_a_im, s5_log_dt,
           s5_b_re, s5_b_im, s5_c_re, s5_c_im, s5_d, s5_w_glu, hgrn_lb_logits, hgrn_norm_g,
           w_up_s5, w_up_hgrn, w_out, router_w, router_b, moe_w1, moe_b1, moe_w2, moe_b2, final_g):
    nb, seq, d = x.shape
    nctx_tok = ctx.shape[1]
    depth = mod_w.shape[0]
    assert d == D and seq % (GRID_W * HG_C) == 0 and nctx_tok % TM == 0 and nb + 1 <= 8
    assert nctx_tok % (8 * S5_T) == 0 and seq % (8 * S5_T) == 0
    n_lat = nb * seq
    n = n_lat + nb * nctx_tok
    tiles_per_batch = seq // TM
    n_lat_tiles = n_lat // TM

    def mod_row(i):
        return jnp.where(i < n_lat_tiles, i // tiles_per_batch, nb)

    cpad = jnp.zeros((8, D), F32).at[:nb].set(c).at[nb].set(c_ctx)
    mods = _modulation(cpad, mod_w, mod_b)
    p_lb = jax.nn.softmax(hgrn_lb_logits.astype(F32), axis=1)
    lb_all = jnp.cumsum(p_lb, axis=1)[:, :depth]

    n_rows = pl.cdiv(n * TOP_K, MOE_BLK) * MOE_BLK + N_EXP * MOE_BLK
    n_blocks = n_rows // MOE_BLK

    rows = seq // GRID_W
    b1_all = moe_b1.reshape(depth, N_EXP, 1, 2 * D_FF)
    b2_all = moe_b2.reshape(depth, N_EXP, 1, D)

    def regrid(xa, to_colmajor):
        shape = (nb, rows, GRID_W, D) if to_colmajor else (nb, GRID_W, rows, D)
        lat = xa[:n_lat].reshape(shape).transpose(0, 2, 1, 3).reshape(n_lat, D)
        return xa.at[:n_lat].set(lat)

    s5_mats = jax.vmap(_s5_matrices)(s5_a_re, s5_a_im, s5_log_dt, s5_b_re, s5_b_im, s5_c_re, s5_c_im)
    w_in_bf, wglu_bf, wus_bf = w_in.astype(BF16), s5_w_glu.astype(BF16), w_up_s5.astype(BF16)
    wuh_bf, wo_bf = w_up_hgrn.astype(BF16), w_out.astype(BF16)
    rw_all = jnp.zeros((depth, D, LANES), F32).at[:, :, :N_EXP].set(router_w)
    rwh_all = rw_all.astype(BF16)
    rwl_all = (rw_all - rwh_all.astype(F32)).astype(BF16)
    rb_all = jnp.full((depth, 1, LANES), NEG, F32).at[:, 0, :N_EXP].set(router_b)

    xall = jnp.concatenate([x.reshape(n_lat, D), ctx.reshape(nb * nctx_tok, D)], axis=0)
    xs = jnp.zeros((n_rows, D), F32)
    colmajor = False
    for l in range(depth):
        if (l % 2 == 1) != colmajor:
            xall = regrid(xall, not colmajor)
            colmajor = not colmajor
        mods3 = mods[l, :nb + 1].reshape(nb + 1, 1, N_MOD * D)
        z, u_bf = _inproj(xall, mods3, norm1_g[l], w_in_bf[l], mod_row)
        y5s = _s5_branch(u_bf, nb, seq, nctx_tok, tuple(m[l] for m in s5_mats))
        hg_out = _hgrn_branch(z, nb, seq, nctx_tok, lb_all[:, l])
        params = dict(d=s5_d[l].reshape(1, D_S5), wglu=wglu_bf[l],
                      hn=hgrn_norm_g[l].reshape(1, HG_DK), wus=wus_bf[l],
                      wuh=wuh_bf[l], wo=wo_bf[l],
                      n2=norm2_g[l].reshape(1, D), rwh=rwh_all[l], rwl=rwl_all[l], rb=rb_all[l])
        xn, h2, gate, meta, cnt = _post(xall, z, y5s, hg_out, mods3, mod_row, params, n_lat_tiles)
        counts = cnt[0, :N_EXP].astype(jnp.int32)
        padded = (counts + MOE_BLK - 1) // MOE_BLK * MOE_BLK
        pad_end = jnp.cumsum(padded)
        pad_start = pad_end - padded
        dest = pad_start[meta[:, :TOP_K]] + meta[:, TOP_K:2 * TOP_K]
        dest3 = dest.reshape(n // TM, 1, TM * TOP_K)
        blk_start = jnp.arange(n_blocks, dtype=jnp.int32) * MOE_BLK
        block_e = jnp.minimum(
            jnp.sum((pad_end[None, :] <= blk_start[:, None]).astype(jnp.int32), axis=1), N_EXP - 1)
        n_used = (pad_end[-1:] // MOE_BLK).astype(jnp.int32)
        xs = _dispatch(h2, dest3, xs)
        ys = _experts(block_e, n_used, xs, l, moe_w1, b1_all, moe_w2, b2_all)
        xall = _combine(xn, gate, dest3, ys, mods3, mod_row)
    if colmajor:
        xall = regrid(xall, False)
    return _final_norm(xall, final_g, n_lat).reshape(nb, seq, D)
```
